```python
import jax, jax.numpy as jnp
from jax import lax
import numpy as np

D_MODEL = 2048
BATCH = 1
SEQ = 8192
DEPTH = 2

GRID_W = 64
ROPE_THETA = 10000.0
Q_BLOCK = 128
D_MIX = D_MODEL
GQA_HEAD_DIM = 64
GQA_HEADS = (D_MIX // 2) // GQA_HEAD_DIM
GQA_KV_HEADS = GQA_HEADS // 4
MLA_V_DIM = 128
MLA_HEADS = (D_MIX // 4) // MLA_V_DIM
MLA_NOPE_DIM = 128
MLA_ROPE_DIM = 64
MLA_Q_RANK = D_MODEL // 4
MLA_KV_RANK = D_MODEL // 8
MLSTM_HEAD_DIM = 128
MLSTM_HEADS = (D_MIX // 4) // MLSTM_HEAD_DIM
MLSTM_WIDTH = MLSTM_HEADS * MLSTM_HEAD_DIM
MLSTM_CHUNK = 128
CONV_W = 3
DENSE_FF = ((8 * D_MODEL // 3 + 255) // 256) * 256
N_EXPERTS = 8
TOP_K = 2
EXPERT_FF = 7 * D_MODEL // 2
DEEPNORM_ALPHA = (2 * DEPTH) ** 0.25
DEEPNORM_BETA = (8 * DEPTH) ** -0.25
RMS_EPS = 1e-6
LN_EPS = 1e-5
IN_SIZES = (GQA_HEADS * GQA_HEAD_DIM, GQA_KV_HEADS * GQA_HEAD_DIM, GQA_KV_HEADS * GQA_HEAD_DIM,
            MLA_Q_RANK, MLA_KV_RANK, MLA_ROPE_DIM,
            MLSTM_WIDTH, MLSTM_WIDTH, MLSTM_WIDTH, MLSTM_WIDTH, 4 * MLSTM_HEADS)
D_IN = sum(IN_SIZES)
IN_SPLITS = tuple(int(s) for s in np.cumsum(IN_SIZES)[:-1])
MIX_OUT = GQA_HEADS * GQA_HEAD_DIM + MLA_HEADS * MLA_V_DIM + MLSTM_WIDTH

kernel_name = 'hybrid_gqa_mla_mlstm_moe_encoder'


def rms_norm(x, g):
    xf = x.astype(jnp.float32)
    y = xf * lax.rsqrt(jnp.mean(xf * xf, axis=-1, keepdims=True) + RMS_EPS)
    return (y * g.astype(jnp.float32)).astype(x.dtype)


def layer_norm(x, g, b):
    xf = x.astype(jnp.float32)
    mu = jnp.mean(xf, axis=-1, keepdims=True)
    var = jnp.mean(jnp.square(xf - mu), axis=-1, keepdims=True)
    y = (xf - mu) * lax.rsqrt(var + LN_EPS)
    return (y * g.astype(jnp.float32) + b.astype(jnp.float32)).astype(x.dtype)


def axial_rope(seq_len, dim):
    rows = seq_len // GRID_W
    row_pos = jnp.broadcast_to(jnp.arange(rows, dtype=jnp.float32)[:, None], (rows, GRID_W)).reshape(-1)
    col_pos = jnp.broadcast_to(jnp.arange(GRID_W, dtype=jnp.float32)[None, :], (rows, GRID_W)).reshape(-1)
    n_freq = dim // 4
    inv_freq = ROPE_THETA ** (-jnp.arange(n_freq, dtype=jnp.float32) / n_freq)
    ang_r = row_pos[:, None] * inv_freq
    ang_c = col_pos[:, None] * inv_freq
    ang = jnp.concatenate([ang_r, ang_r, ang_c, ang_c], axis=-1)
    return jnp.cos(ang), jnp.sin(ang)


def apply_rope(x, cos, sin):
    xf = x.astype(jnp.float32)
    a1, a2, b1, b2 = jnp.split(xf, 4, axis=-1)
    rot = jnp.concatenate([-a2, a1, -b2, b1], axis=-1)
    return (xf * cos[:, None, :] + rot * sin[:, None, :]).astype(x.dtype)


def blocked_attention(q, k, v):
    B, S, Hk, G, Dq = q.shape
    Dv = v.shape[-1]
    scale = Dq ** -0.5
    qb = jnp.moveaxis(q.reshape(B, S // Q_BLOCK, Q_BLOCK, Hk, G, Dq), 1, 0)

    def one_block(q_blk):
        s = jnp.einsum('bqkgd,bskd->bkgqs', q_blk, k).astype(jnp.float32) * scale
        p = jax.nn.softmax(s, axis=-1).astype(v.dtype)
        return jnp.einsum('bkgqs,bskd->bqkgd', p, v)

    out = lax.map(one_block, qb)
    return jnp.moveaxis(out, 0, 1).reshape(B, S, Hk * G * Dv)


def mlstm_chunkwise(q, k, v, log_i, log_f):
    B, H, S, D = q.shape
    L = MLSTM_CHUNK
    nc = S // L
    q, k, v = (t.reshape(B, H, nc, L, D) for t in (q, k, v))
    li = log_i.reshape(B, H, nc, L)
    b = jnp.cumsum(log_f.reshape(B, H, nc, L), axis=-1)
    g = b[..., -1]
    a = g[..., None] - b + li
    m_loc = jnp.max(a, axis=-1)
    w = jnp.exp(a - m_loc[..., None])
    kv = jnp.einsum('bhcl,bhcld,bhcle->bhcde', w, k, v)
    ks = jnp.einsum('bhcl,bhcld->bhcd', w, k)

    def step(carry, inp):
        C, n, m = carry
        g_c, m_c, kv_c, ks_c = inp
        m_new = jnp.maximum(g_c + m, m_c)
        decay = jnp.exp(g_c + m - m_new)
        add = jnp.exp(m_c - m_new)
        C_new = decay[..., None, None] * C + add[..., None, None] * kv_c
        n_new = decay[..., None] * n + add[..., None] * ks_c
        return (C_new, n_new, m_new), (C, n, m)

    init = (jnp.zeros((B, H, D, D), jnp.float32), jnp.zeros((B, H, D), jnp.float32), jnp.zeros((B, H), jnp.float32))
    xs = tuple(jnp.moveaxis(t, 2, 0) for t in (g, m_loc, kv, ks))
    _, (C_prev, n_prev, m_prev) = lax.scan(step, init, xs)
    C_prev = jnp.moveaxis(C_prev, 0, 2)
    n_prev = jnp.moveaxis(n_prev, 0, 2)
    m_prev = jnp.moveaxis(m_prev, 0, 2)

    lower = jnp.tril(jnp.ones((L, L), dtype=bool))
    log_d = jnp.where(lower, b[..., :, None] - b[..., None, :] + li[..., None, :], -jnp.inf)
    inter = b + m_prev[..., None]
    m_t = jnp.maximum(inter, jnp.max(log_d, axis=-1))
    d_mat = jnp.exp(log_d - m_t[..., None])
    inter_w = jnp.exp(inter - m_t)
    qk = jnp.einsum('bhctd,bhcsd->bhcts', q, k) * d_mat
    num = jnp.einsum('bhcts,bhcsd->bhctd', qk, v) + inter_w[..., None] * jnp.einsum('bhctd,bhcde->bhcte', q, C_prev)
    den = jnp.sum(qk, axis=-1) + inter_w * jnp.einsum('bhctd,bhcd->bhct', q, n_prev)
    h = num / jnp.maximum(jnp.abs(den), jnp.exp(-m_t))[..., None]
    return h.reshape(B, H, S, D)


def mlstm_mixer(mq, mk, mv, mo, mg, mlstm_conv, mlstm_gate_b, mlstm_head_norm):
    B, S, _ = mq.shape
    qk = lax.conv_general_dilated(jnp.concatenate([mq, mk], axis=-1), mlstm_conv[:, None, :],
                                  window_strides=(1,), padding='SAME',
                                  dimension_numbers=('NWC', 'WIO', 'NWC'),
                                  feature_group_count=2 * MLSTM_WIDTH)
    qc, kc = jnp.split(jax.nn.silu(qk), 2, axis=-1)

    def heads(t):
        return t.astype(jnp.float32).reshape(B, S, MLSTM_HEADS, MLSTM_HEAD_DIM).transpose(0, 2, 1, 3)

    q, k, v = heads(qc), heads(kc) * (MLSTM_HEAD_DIM ** -0.5), heads(mv)
    pre = (mg.astype(jnp.float32) + mlstm_gate_b.astype(jnp.float32)).reshape(B, S, 4, MLSTM_HEADS).transpose(2, 0, 3, 1)
    i_fw, f_fw, i_bw, f_bw = pre[0], pre[1], pre[2], pre[3]

    def rev(t):
        return jnp.flip(t, axis=2)

    h_fw = mlstm_chunkwise(q, k, v, i_fw, jax.nn.log_sigmoid(f_fw))
    h_bw = rev(mlstm_chunkwise(rev(q), rev(k), rev(v), rev(i_bw), rev(jax.nn.log_sigmoid(f_bw))))
    h = (h_fw + h_bw).transpose(0, 2, 1, 3)
    mu = jnp.mean(h, axis=-1, keepdims=True)
    var = jnp.mean(jnp.square(h - mu), axis=-1, keepdims=True)
    hn = ((h - mu) * lax.rsqrt(var + LN_EPS)).reshape(B, S, MLSTM_WIDTH) * mlstm_head_norm.astype(jnp.float32)
    return (jax.nn.sigmoid(mo.astype(jnp.float32)) * hn).astype(mq.dtype)


def token_mixer(u, cos, sin, w_in, gqa_q_norm, gqa_k_norm, mla_q_norm, mla_w_uq, mla_kv_norm, mla_w_ukv,
                mlstm_conv, mlstm_gate_b, mlstm_head_norm, w_out):
    B, S, _ = u.shape
    gq, gk, gv, cq, ckv, kr, mq, mk, mv, mo, mg = jnp.split(u @ w_in, IN_SPLITS, axis=-1)

    q = apply_rope(rms_norm(gq.reshape(B, S, GQA_HEADS, GQA_HEAD_DIM), gqa_q_norm), cos, sin)
    k = apply_rope(rms_norm(gk.reshape(B, S, GQA_KV_HEADS, GQA_HEAD_DIM), gqa_k_norm), cos, sin)
    v = gv.reshape(B, S, GQA_KV_HEADS, GQA_HEAD_DIM)
    q = q.reshape(B, S, GQA_KV_HEADS, GQA_HEADS // GQA_KV_HEADS, GQA_HEAD_DIM)
    out_a = blocked_attention(q, k, v)

    qm = (rms_norm(cq, mla_q_norm) @ mla_w_uq).reshape(B, S, MLA_HEADS, MLA_NOPE_DIM + MLA_ROPE_DIM)
    q_nope, q_rope = jnp.split(qm, [MLA_NOPE_DIM], axis=-1)
    kvm = (rms_norm(ckv, mla_kv_norm) @ mla_w_ukv).reshape(B, S, MLA_HEADS, MLA_NOPE_DIM + MLA_V_DIM)
    k_nope, v_m = jnp.split(kvm, [MLA_NOPE_DIM], axis=-1)
    k_rope = apply_rope(kr.reshape(B, S, 1, MLA_ROPE_DIM), cos, sin)
    q_m = jnp.concatenate([q_nope, apply_rope(q_rope, cos, sin)], axis=-1)[:, :, :, None, :]
    k_m = jnp.concatenate([k_nope, jnp.broadcast_to(k_rope, (B, S, MLA_HEADS, MLA_ROPE_DIM))], axis=-1)
    out_b = blocked_attention(q_m, k_m, v_m)

    out_c = mlstm_mixer(mq, mk, mv, mo, mg, mlstm_conv, mlstm_gate_b, mlstm_head_norm)

    return jnp.concatenate([out_a, out_b, out_c], axis=-1) @ w_out


def dense_swiglu(u, w_gate, w_up, w_down):
    return (jax.nn.silu(u @ w_gate) * (u @ w_up)) @ w_down


def moe_swiglu(u, router_w, w_gate, w_up, w_down):
    logits = (u @ router_w).astype(jnp.float32)
    top_v, top_i = lax.top_k(logits, TOP_K)
    top_w = jax.nn.softmax(top_v, axis=-1)
    gates = jnp.sum(jax.nn.one_hot(top_i, N_EXPERTS, dtype=jnp.float32) * top_w[..., None], axis=-2).astype(u.dtype)
    y = jnp.zeros_like(u)
    for e in range(N_EXPERTS):
        h = jax.nn.silu(u @ w_gate[e]) * (u @ w_up[e])
        y = y + gates[..., e:e + 1] * (h @ w_down[e])
    return y


def setup_inputs(seed: int = 0) -> dict:
    key = jax.random.key(seed)
    keys = iter(jax.random.split(key, 128))

    def nrm(shape, std):
        return jax.random.normal(next(keys), shape, jnp.float32) * std

    def gain(n):
        return 1.0 + nrm((n,), 0.02)

    inp = {'x': nrm((BATCH, SEQ, D_MODEL), 1.0), 'c': nrm((BATCH, D_MODEL), 1.0)}
    f_bias = jnp.linspace(3.0, 6.0, MLSTM_HEADS, dtype=jnp.float32)
    for i in range(DEPTH):
        p = 'l%d_' % i
        inp[p + 'ada_w'] = nrm((D_MODEL, 6 * D_MODEL), 0.5 * D_MODEL ** -0.5)
        inp[p + 'ada_b'] = nrm((6 * D_MODEL,), 0.02)
        inp[p + 'w_in'] = nrm((D_MODEL, D_IN), D_MODEL ** -0.5)
        inp[p + 'gqa_q_norm'] = gain(GQA_HEAD_DIM)
        inp[p + 'gqa_k_norm'] = gain(GQA_HEAD_DIM)
        inp[p + 'mla_q_norm'] = gain(MLA_Q_RANK)
        inp[p + 'mla_w_uq'] = nrm((MLA_Q_RANK, MLA_HEADS * (MLA_NOPE_DIM + MLA_ROPE_DIM)), MLA_Q_RANK ** -0.5)
        inp[p + 'mla_kv_norm'] = gain(MLA_KV_RANK)
        inp[p + 'mla_w_ukv'] = nrm((MLA_KV_RANK, MLA_HEADS * (MLA_NOPE_DIM + MLA_V_DIM)), MLA_KV_RANK ** -0.5)
        inp[p + 'mlstm_conv'] = nrm((CONV_W, 2 * MLSTM_WIDTH), CONV_W ** -0.5)
        inp[p + 'mlstm_gate_b'] = jnp.concatenate([
            nrm((MLSTM_HEADS,), 0.1), f_bias + nrm((MLSTM_HEADS,), 0.1),
            nrm((MLSTM_HEADS,), 0.1), f_bias + nrm((MLSTM_HEADS,), 0.1)])
        inp[p + 'mlstm_head_norm'] = gain(MLSTM_WIDTH)
        inp[p + 'w_out'] = nrm((MIX_OUT, D_MODEL), DEEPNORM_BETA * MIX_OUT ** -0.5)
        inp[p + 'ln1_g'] = gain(D_MODEL)
        inp[p + 'ln1_b'] = nrm((D_MODEL,), 0.02)
        if i % 2 == 0:
            inp[p + 'ffn_w_gate'] = nrm((D_MODEL, DENSE_FF), D_MODEL ** -0.5)
            inp[p + 'ffn_w_up'] = nrm((D_MODEL, DENSE_FF), D_MODEL ** -0.5)
            inp[p + 'ffn_w_down'] = nrm((DENSE_FF, D_MODEL), DEEPNORM_BETA * DENSE_FF ** -0.5)
        else:
            inp[p + 'router_w'] = nrm((D_MODEL, N_EXPERTS), D_MODEL ** -0.5)
            inp[p + 'moe_w_gate'] = nrm((N_EXPERTS, D_MODEL, EXPERT_FF), D_MODEL ** -0.5)
            inp[p + 'moe_w_up'] = nrm((N_EXPERTS, D_MODEL, EXPERT_FF), D_MODEL ** -0.5)
            inp[p + 'moe_w_down'] = nrm((N_EXPERTS, EXPERT_FF, D_MODEL), DEEPNORM_BETA * EXPERT_FF ** -0.5)
        inp[p + 'ln2_g'] = gain(D_MODEL)
        inp[p + 'ln2_b'] = nrm((D_MODEL,), 0.02)
    return inp


def reference(x, c,
              l0_ada_w, l0_ada_b, l0_w_in, l0_gqa_q_norm, l0_gqa_k_norm, l0_mla_q_norm, l0_mla_w_uq,
              l0_mla_kv_norm, l0_mla_w_ukv, l0_mlstm_conv, l0_mlstm_gate_b, l0_mlstm_head_norm, l0_w_out,
              l0_ln1_g, l0_ln1_b, l0_ffn_w_gate, l0_ffn_w_up, l0_ffn_w_down, l0_ln2_g, l0_ln2_b,
              l1_ada_w, l1_ada_b, l1_w_in, l1_gqa_q_norm, l1_gqa_k_norm, l1_mla_q_norm, l1_mla_w_uq,
              l1_mla_kv_norm, l1_mla_w_ukv, l1_mlstm_conv, l1_mlstm_gate_b, l1_mlstm_head_norm, l1_w_out,
              l1_ln1_g, l1_ln1_b, l1_router_w, l1_moe_w_gate, l1_moe_w_up, l1_moe_w_down, l1_ln2_g, l1_ln2_b):
    ada_params = ((l0_ada_w, l0_ada_b), (l1_ada_w, l1_ada_b))
    mixer_params = (
        (l0_w_in, l0_gqa_q_norm, l0_gqa_k_norm, l0_mla_q_norm, l0_mla_w_uq, l0_mla_kv_norm, l0_mla_w_ukv,
         l0_mlstm_conv, l0_mlstm_gate_b, l0_mlstm_head_norm, l0_w_out),
        (l1_w_in, l1_gqa_q_norm, l1_gqa_k_norm, l1_mla_q_norm, l1_mla_w_uq, l1_mla_kv_norm, l1_mla_w_ukv,
         l1_mlstm_conv, l1_mlstm_gate_b, l1_mlstm_head_norm, l1_w_out))
    ffn_params = ((l0_ffn_w_gate, l0_ffn_w_up, l0_ffn_w_down),
                  (l1_router_w, l1_moe_w_gate, l1_moe_w_up, l1_moe_w_down))
    norm_params = ((l0_ln1_g, l0_ln1_b, l0_ln2_g, l0_ln2_b), (l1_ln1_g, l1_ln1_b, l1_ln2_g, l1_ln2_b))

    cos, sin = axial_rope(x.shape[1], GQA_HEAD_DIM)
    c_act = jax.nn.silu(c)
    for i in range(DEPTH):
        ada_w, ada_b = ada_params[i]
        ln1_g, ln1_b, ln2_g, ln2_b = norm_params[i]
        mod = (c_act @ ada_w + ada_b)[:, None, :]
        shift1, scale1, gate1, shift2, scale2, gate2 = jnp.split(mod, 6, axis=-1)
        u = x * (1.0 + scale1) + shift1
        x = layer_norm(DEEPNORM_ALPHA * x + gate1 * token_mixer(u, cos, sin, *mixer_params[i]), ln1_g, ln1_b)
        u = x * (1.0 + scale2) + shift2
        if i % 2 == 0:
            f = dense_swiglu(u, *ffn_params[i])
        else:
            f = moe_swiglu(u, *ffn_params[i])
        x = layer_norm(DEEPNORM_ALPHA * x + gate2 * f, ln2_g, ln2_b)
    return x
```

```python
import functools

import jax
import jax.numpy as jnp
from jax import lax
from jax.experimental import pallas as pl
from jax.experimental.pallas import tpu as pltpu

F32 = jnp.float32
BF16 = jnp.bfloat16

GRID_W = 64
ROPE_THETA = 10000.0
HEAD64 = 64
GQA_HEADS = 16
GQA_KV = 4
MLA_HEADS = 4
MLA_NOPE = 128
MLA_V = 128
MLA_QR = 512
MLA_KVR = 256
ML_HEADS = 4
ML_D = 128
ML_W = ML_HEADS * ML_D
CHUNK = 128
N_EXPERTS = 8
RMS_EPS = 1e-6
LN_EPS = 1e-5
DEPTH = 2
ALPHA = (2 * DEPTH) ** 0.25

LANES = 128
NEG_BIG = -1e30
VMEM_LIMIT = 56 * 1024 * 1024


def _cp(*sem, vmem=VMEM_LIMIT):
    return pltpu.CompilerParams(dimension_semantics=sem, vmem_limit_bytes=vmem)


def _full(shape):
    n = len(shape)
    return pl.BlockSpec(shape, lambda *_: (0,) * n)


def _bf(x):
    return x.astype(BF16)


def _dot(a, b):
    return jnp.dot(a, b, preferred_element_type=F32)


def _split3(x):
    hi = _bf(x)
    r1 = x - hi.astype(F32)
    mid = _bf(r1)
    lo = _bf(r1 - mid.astype(F32))
    return hi, mid, lo


def _ada_kernel(c_ref, w_ref, b_ref, o_ref):
    c = c_ref[...]
    ca = c * jax.nn.sigmoid(c)
    o_ref[...] = jnp.sum(ca * w_ref[...], axis=0, keepdims=True) + b_ref[...]


def _ada(c_col, w, b):
    d, n = w.shape
    tn = 1024
    return pl.pallas_call(
        _ada_kernel,
        out_shape=jax.ShapeDtypeStruct((1, n), F32),
        grid=(n // tn,),
        in_specs=[_full((d, 1)), pl.BlockSpec((d, tn), lambda j: (0, j)), pl.BlockSpec((1, tn), lambda j: (0, j))],
        out_specs=pl.BlockSpec((1, tn), lambda j: (0, j)),
        compiler_params=_cp("parallel"),
        name="ada_mod",
    )(c_col, w, b.reshape(1, n))


def _group_sumsq(blk, bd):
    hi, mid, lo = _split3(blk * blk)
    return _dot(hi, bd) + _dot(mid, bd) + _dot(lo, bd)


def _rope(blk, cos, sin_a, sin_b):
    return blk * cos + pltpu.roll(blk, LANES - 16, 1) * sin_a + pltpu.roll(blk, 16, 1) * sin_b


def _modulate(x, scale, shift):
    return x * (1.0 + scale) + shift


def _inproj_gqa_kernel(x_ref, sc_ref, sh_ref, w_ref, qg_ref, kg_ref, cos_ref, sa_ref, sb_ref, bd_ref,
                       qT_ref, k_ref, vT_ref):
    u = _bf(_modulate(x_ref[...], sc_ref[...], sh_ref[...]))
    y = _dot(u, w_ref[...])
    cos, sa, sb, bd = cos_ref[...], sa_ref[...], sb_ref[...], bd_ref[...]
    nq = GQA_HEADS * HEAD64 // LANES
    nk = GQA_KV * HEAD64 // LANES
    scale = HEAD64 ** -0.5
    for j in range(nq):
        blk = y[:, j * LANES:(j + 1) * LANES]
        ss = _group_sumsq(blk, bd)
        blk = blk * lax.rsqrt(ss * (1.0 / HEAD64) + RMS_EPS) * qg_ref[...]
        blk = _rope(blk, cos, sa, sb) * scale
        qT_ref[j * LANES:(j + 1) * LANES, :] = _bf(blk.T)
    for j in range(nk):
        blk = y[:, (nq + j) * LANES:(nq + j + 1) * LANES]
        ss = _group_sumsq(blk, bd)
        blk = blk * lax.rsqrt(ss * (1.0 / HEAD64) + RMS_EPS) * kg_ref[...]
        blk = _bf(_rope(blk, cos, sa, sb))
        k_ref[2 * j] = blk[:, :HEAD64]
        k_ref[2 * j + 1] = blk[:, HEAD64:]
    v = y[:, (nq + nk) * LANES:]
    vT_ref[...] = _bf(v.T)


def _inproj_gqa(x, sc, sh, w, qg, kg, cos, sa, sb, bd, tm=256):
    s, d = x.shape
    n = w.shape[1]
    row = lambda i: (i, 0)
    return pl.pallas_call(
        _inproj_gqa_kernel,
        out_shape=(jax.ShapeDtypeStruct((GQA_HEADS * HEAD64, s), BF16),
                   jax.ShapeDtypeStruct((GQA_KV, s, HEAD64), BF16),
                   jax.ShapeDtypeStruct((GQA_KV * HEAD64, s), BF16)),
        grid=(s // tm,),
        in_specs=[pl.BlockSpec((tm, d), row), _full((1, d)), _full((1, d)), _full((d, n)),
                  _full((1, LANES)), _full((1, LANES)),
                  pl.BlockSpec((tm, LANES), row), pl.BlockSpec((tm, LANES), row), pl.BlockSpec((tm, LANES), row),
                  _full((LANES, LANES))],
        out_specs=(pl.BlockSpec((GQA_HEADS * HEAD64, tm), lambda i: (0, i)),
                   pl.BlockSpec((GQA_KV, tm, HEAD64), lambda i: (0, i, 0)),
                   pl.BlockSpec((GQA_KV * HEAD64, tm), lambda i: (0, i))),
        compiler_params=_cp("parallel"),
        name="inproj_gqa",
    )(x, sc, sh, w, qg, kg, cos, sa, sb, bd)


def _inproj_mla_kernel(x_ref, sc_ref, sh_ref, w_ref, qn_ref, kvn_ref, wuq_ref, wukv_ref, cos_ref, sa_ref, sb_ref,
                       qT_ref, k_ref, vT_ref):
    u = _bf(_modulate(x_ref[...], sc_ref[...], sh_ref[...]))
    y = _dot(u, w_ref[...])
    cos, sa, sb = cos_ref[...], sa_ref[...], sb_ref[...]
    cq = y[:, :MLA_QR]
    ckv = y[:, MLA_QR:MLA_QR + MLA_KVR]
    kr = y[:, MLA_QR + MLA_KVR:]
    cqn = cq * lax.rsqrt(jnp.mean(cq * cq, axis=-1, keepdims=True) + RMS_EPS) * qn_ref[...]
    ckvn = ckv * lax.rsqrt(jnp.mean(ckv * ckv, axis=-1, keepdims=True) + RMS_EPS) * kvn_ref[...]
    qm = _dot(_bf(cqn), wuq_ref[...])
    kvm = _dot(_bf(ckvn), wukv_ref[...])
    scale = (MLA_NOPE + HEAD64) ** -0.5
    dq = MLA_NOPE + HEAD64
    k_rope = _bf(_rope(kr, cos, sa, sb)[:, :HEAD64])
    for p in range(MLA_HEADS // 2):
        rblk = qm[:, MLA_HEADS * MLA_NOPE + p * LANES:MLA_HEADS * MLA_NOPE + (p + 1) * LANES]
        rT = _bf((_rope(rblk, cos, sa, sb) * scale).T)
        for e in range(2):
            h = 2 * p + e
            qT_ref[h * dq + MLA_NOPE:(h + 1) * dq, :] = rT[e * HEAD64:(e + 1) * HEAD64, :]
    for h in range(MLA_HEADS):
        nope = qm[:, h * MLA_NOPE:(h + 1) * MLA_NOPE] * scale
        qT_ref[h * dq:h * dq + MLA_NOPE, :] = _bf(nope.T)
        k_ref[h, :, :MLA_NOPE] = _bf(kvm[:, h * 2 * LANES:h * 2 * LANES + MLA_NOPE])
        k_ref[h, :, MLA_NOPE:] = k_rope
        vT_ref[h * MLA_V:(h + 1) * MLA_V, :] = _bf(kvm[:, h * 2 * LANES + MLA_NOPE:(h + 1) * 2 * LANES].T)


def _inproj_mla(x, sc, sh, w, qn, kvn, wuq, wukv, cos, sa, sb, tm=256):
    s, d = x.shape
    n = w.shape[1]
    dq = MLA_NOPE + HEAD64
    row = lambda i: (i, 0)
    return pl.pallas_call(
        _inproj_mla_kernel,
        out_shape=(jax.ShapeDtypeStruct((MLA_HEADS * dq, s), BF16),
                   jax.ShapeDtypeStruct((MLA_HEADS, s, dq), BF16),
                   jax.ShapeDtypeStruct((MLA_HEADS * MLA_V, s), BF16)),
        grid=(s // tm,),
        in_specs=[pl.BlockSpec((tm, d), row), _full((1, d)), _full((1, d)), _full((d, n)),
                  _full((1, MLA_QR)), _full((1, MLA_KVR)), _full(wuq.shape), _full(wukv.shape),
                  pl.BlockSpec((tm, LANES), row), pl.BlockSpec((tm, LANES), row), pl.BlockSpec((tm, LANES), row)],
        out_specs=(pl.BlockSpec((MLA_HEADS * dq, tm), lambda i: (0, i)),
                   pl.BlockSpec((MLA_HEADS, tm, dq), lambda i: (0, i, 0)),
                   pl.BlockSpec((MLA_HEADS * MLA_V, tm), lambda i: (0, i))),
        compiler_params=_cp("parallel"),
        name="inproj_mla",
    )(x, sc, sh, w, qn, kvn, wuq, wukv, cos, sa, sb)


def _inproj_mlstm_kernel(x_ref, xp_ref, xn_ref, sc_ref, sh_ref, w_ref, conv_ref, gb_ref,
                         q_ref, kT_ref, v_ref, o_ref, gT_ref):
    i = pl.program_id(0)
    last = pl.num_programs(0) - 1
    sc, sh = sc_ref[...], sh_ref[...]
    u = _bf(_modulate(x_ref[...], sc, sh))
    y = _dot(u, w_ref[...])
    tm = y.shape[0]
    qk = y[:, :2 * ML_W]
    halo = jnp.concatenate([xp_ref[...], xn_ref[...]], axis=0)
    hy = _dot(_bf(_modulate(halo, sc, sh)), w_ref[:, :2 * ML_W])
    prev_row = jnp.where(i == 0, 0.0, hy[7:8, :])
    next_row = jnp.where(i == last, 0.0, hy[8:9, :])
    rows = lax.broadcasted_iota(jnp.int32, qk.shape, 0)
    prev = jnp.where(rows == 0, prev_row, pltpu.roll(qk, 1, 0))
    nxt = jnp.where(rows == tm - 1, next_row, pltpu.roll(qk, tm - 1, 0))
    cw = conv_ref[...]
    z = cw[0:1, :] * prev + cw[1:2, :] * qk + cw[2:3, :] * nxt
    z = z * jax.nn.sigmoid(z)
    q_ref[...] = _bf(z[:, :ML_W])
    kT_ref[...] = (z[:, ML_W:] * (ML_D ** -0.5)).T
    v_ref[...] = _bf(y[:, 2 * ML_W:3 * ML_W])
    o_ref[...] = y[:, 3 * ML_W:4 * ML_W]
    g = y[:, 4 * ML_W:] + gb_ref[...]
    gT_ref[...] = g.T[:4 * ML_HEADS, :]


def _inproj_mlstm(x, sc, sh, w, conv, gb, tm=256):
    s, d = x.shape
    n = w.shape[1]
    row = lambda i: (i, 0)
    col = lambda i: (0, i)
    r8 = tm // 8
    return pl.pallas_call(
        _inproj_mlstm_kernel,
        out_shape=(jax.ShapeDtypeStruct((s, ML_W), BF16), jax.ShapeDtypeStruct((ML_W, s), F32),
                   jax.ShapeDtypeStruct((s, ML_W), BF16), jax.ShapeDtypeStruct((s, ML_W), F32),
                   jax.ShapeDtypeStruct((4 * ML_HEADS, s), F32)),
        grid=(s // tm,),
        in_specs=[pl.BlockSpec((tm, d), row),
                  pl.BlockSpec((8, d), lambda i: (jnp.maximum(i * r8 - 1, 0), 0)),
                  pl.BlockSpec((8, d), lambda i: (jnp.minimum((i + 1) * r8, s // 8 - 1), 0)),
                  _full((1, d)), _full((1, d)), _full((d, n)), _full((3, 2 * ML_W)), _full((1, LANES))],
        out_specs=(pl.BlockSpec((tm, ML_W), row), pl.BlockSpec((ML_W, tm), col),
                   pl.BlockSpec((tm, ML_W), row), pl.BlockSpec((tm, ML_W), row),
                   pl.BlockSpec((4 * ML_HEADS, tm), col)),
        compiler_params=_cp("parallel"),
        name="inproj_mlstm",
    )(x, x, x, sc, sh, w, conv, gb)


def _attn_kernel(qT_ref, k_ref, vT_ref, o_ref, m_sc, l_sc, acc_sc, *, groups, dq, dv, tk):
    seq = k_ref.shape[1]
    if groups == 1:
        qT = qT_ref[...]
    else:
        qT = jnp.concatenate([qT_ref[g * dq:(g + 1) * dq, :] for g in range(groups)], axis=1)
    m_sc[...] = jnp.full(m_sc.shape, NEG_BIG, F32)
    l_sc[...] = jnp.zeros(l_sc.shape, F32)
    acc_sc[...] = jnp.zeros(acc_sc.shape, F32)

    def body(c, carry):
        off = pl.multiple_of(c * tk, tk)
        s = _dot(k_ref[0, pl.ds(off, tk), :], qT)
        m_old = m_sc[...]
        m_new = jnp.maximum(m_old, jnp.max(s, axis=0, keepdims=True))
        alpha = jnp.exp(m_old - m_new)
        p = jnp.exp(s - m_new)
        l_sc[...] = alpha * l_sc[...] + jnp.sum(p, axis=0, keepdims=True)
        acc_sc[...] = alpha * acc_sc[...] + _dot(vT_ref[:, pl.ds(off, tk)], _bf(p))
        m_sc[...] = m_new
        return carry

    lax.fori_loop(0, seq // tk, body, 0)
    oT = acc_sc[...] / l_sc[...]
    c = qT_ref.shape[1]
    if groups > 1:
        oT = jnp.concatenate([oT[:, g * c:(g + 1) * c] for g in range(groups)], axis=0)
    o_ref[...] = _bf(oT.T)


def _attention(qT, k, vT, *, groups, dq, dv, cq, tk=512):
    h, s, _ = k.shape
    nq = groups * cq
    kern = functools.partial(_attn_kernel, groups=groups, dq=dq, dv=dv, tk=tk)
    return pl.pallas_call(
        kern,
        out_shape=jax.ShapeDtypeStruct((s, h * groups * dv), BF16),
        grid=(h, s // cq),
        in_specs=[pl.BlockSpec((groups * dq, cq), lambda a, i: (a, i)),
                  pl.BlockSpec((1, s, dq), lambda a, i: (a, 0, 0)),
                  pl.BlockSpec((dv, s), lambda a, i: (a, 0))],
        out_specs=pl.BlockSpec((cq, groups * dv), lambda a, i: (i, a)),
        scratch_shapes=[pltpu.VMEM((1, nq), F32), pltpu.VMEM((1, nq), F32), pltpu.VMEM((dv, nq), F32)],
        compiler_params=_cp("parallel", "parallel"),
        name="attention_g%d" % groups,
    )(qT, k, vT)


def _mlstm_kernel(q_ref, kT_ref, v_ref, o_ref, g_ref, hn_ref, out_ref,
                  cn_sc, mprev_sc, w_sc, r_sc, lf_sc, g_sc, mloc_sc, hacc_sc):
    head = pl.program_id(0)
    nc = g_ref.shape[1]
    L = CHUNK
    ii = lax.broadcasted_iota(jnp.int32, (L, L), 0)
    jj = lax.broadcasted_iota(jnp.int32, (L, L), 1)
    e0 = jnp.where(lax.broadcasted_iota(jnp.int32, (L, LANES), 1) == 0, 1.0, 0.0).astype(BF16)

    for direction in range(2):
        fw = direction == 0
        mask = (jj <= ii) if fw else (jj >= ii)
        tri = jnp.where((ii <= jj) if fw else (ii >= jj), 1.0, 0.0).astype(BF16)
        li = g_ref[2 * ML_HEADS * direction + head]
        fg = g_ref[2 * ML_HEADS * direction + ML_HEADS + head]
        lf = jnp.minimum(fg, 0.0) - jnp.log1p(jnp.exp(-jnp.abs(fg)))
        hi, mid, lo = _split3(lf)
        b = _dot(hi, tri) + _dot(mid, tri) + _dot(lo, tri)
        gtot = jnp.sum(lf, axis=-1, keepdims=True)
        a = gtot - b + li
        mloc = jnp.max(a, axis=-1, keepdims=True)
        w_sc[...] = jnp.exp(a - mloc)
        r_sc[...] = li - b
        lf_sc[...] = lf
        g_sc[...] = jnp.broadcast_to(gtot, (nc, LANES))
        mloc_sc[...] = jnp.broadcast_to(mloc, (nc, LANES))

        def scan_body(i, carry, fw=fw):
            cn, m = carry
            cc = i if fw else nc - 1 - i
            off = pl.multiple_of(cc * L, L)
            cn_sc[cc] = _bf(cn)
            mprev_sc[pl.ds(cc, 1), :] = jnp.broadcast_to(m, (1, LANES))
            kw = _bf(kT_ref[:, pl.ds(off, L)] * w_sc[pl.ds(cc, 1), :])
            vext = jnp.concatenate([v_ref[pl.ds(off, L), :], e0], axis=1)
            kv = _dot(kw, vext)
            g = g_sc[pl.ds(cc, 1), 0:1]
            ml = mloc_sc[pl.ds(cc, 1), 0:1]
            m_new = jnp.maximum(g + m, ml)
            cn = jnp.exp(g + m - m_new) * cn + jnp.exp(ml - m_new) * kv
            return cn, m_new

        lax.fori_loop(0, nc, scan_body, (jnp.zeros((ML_D, 2 * ML_D), F32), jnp.zeros((1, 1), F32)))

        def out_body(cc, carry, fw=fw, mask=mask):
            off = pl.multiple_of(cc * L, L)
            mp = mprev_sc[pl.ds(cc, 1), 0:1]
            rm = jnp.where(mask, r_sc[pl.ds(cc, 1), :], -jnp.inf)
            mt = jnp.maximum(mp, jnp.max(rm, axis=-1, keepdims=True))
            d = jnp.exp(rm - mt)
            iw = jnp.exp(mp - mt)
            qc = q_ref[pl.ds(off, L), :]
            qk = _dot(qc, _bf(kT_ref[:, pl.ds(off, L)])) * d
            num_i = _dot(_bf(qk), v_ref[pl.ds(off, L), :])
            den_i = jnp.sum(qk, axis=-1, keepdims=True)
            inter = _dot(qc, cn_sc[cc])
            num = num_i + iw * inter[:, :ML_D]
            den = den_i + iw * inter[:, ML_D:ML_D + 1]
            bcol = jnp.sum(jnp.where(mask, lf_sc[pl.ds(cc, 1), :], 0.0), axis=-1, keepdims=True)
            h = num / jnp.maximum(jnp.abs(den), jnp.exp(-(bcol + mt)))
            if fw:
                hacc_sc[pl.ds(off, L), :] = h
            else:
                h = h + hacc_sc[pl.ds(off, L), :]
                mu = jnp.mean(h, axis=-1, keepdims=True)
                hc = h - mu
                var = jnp.mean(hc * hc, axis=-1, keepdims=True)
                hn = hc * lax.rsqrt(var + LN_EPS) * hn_ref[...]
                out_ref[pl.ds(off, L), :] = _bf(jax.nn.sigmoid(o_ref[pl.ds(off, L), :]) * hn)
            return carry

        lax.fori_loop(0, nc, out_body, 0)


def _mlstm(q, kT, v, o, g3, hn):
    s = q.shape[0]
    nc = s // CHUNK
    colblk = lambda h: (0, h)
    return pl.pallas_call(
        _mlstm_kernel,
        out_shape=jax.ShapeDtypeStruct((s, ML_W), BF16),
        grid=(ML_HEADS,),
        in_specs=[pl.BlockSpec((s, ML_D), colblk), pl.BlockSpec((ML_D, s), lambda h: (h, 0)),
                  pl.BlockSpec((s, ML_D), colblk), pl.BlockSpec((s, ML_D), colblk),
                  _full(g3.shape), pl.BlockSpec((1, ML_D), colblk)],
        out_specs=pl.BlockSpec((s, ML_D), colblk),
        scratch_shapes=[pltpu.VMEM((nc, ML_D, 2 * ML_D), BF16), pltpu.VMEM((nc, LANES), F32),
                        pltpu.VMEM((nc, CHUNK), F32), pltpu.VMEM((nc, CHUNK), F32), pltpu.VMEM((nc, CHUNK), F32),
                        pltpu.VMEM((nc, LANES), F32), pltpu.VMEM((nc, LANES), F32),
                        pltpu.VMEM((s, ML_D), F32)],
        compiler_params=_cp("parallel"),
        name="mlstm",
    )(q, kT, v, o, g3, hn)


def _layer_norm(z, g, b):
    mu = jnp.mean(z, axis=-1, keepdims=True)
    zc = z - mu
    var = jnp.mean(zc * zc, axis=-1, keepdims=True)
    return zc * lax.rsqrt(var + LN_EPS) * g + b


def _outproj_kernel(*refs, route):
    if route:
        (x_ref, a_ref, b_ref, c_ref, w_ref, gate_ref, g_ref, bb_ref, sc_ref, sh_ref, rwh_ref, rwl_ref,
         x1_ref, u2_ref, lg_ref) = refs
    else:
        x_ref, a_ref, b_ref, c_ref, w_ref, gate_ref, g_ref, bb_ref, sc_ref, sh_ref, x1_ref, u2_ref = refs
    na, nb = a_ref.shape[1], b_ref.shape[1]
    y = _dot(a_ref[...], w_ref[:na, :]) + _dot(b_ref[...], w_ref[na:na + nb, :]) + _dot(c_ref[...], w_ref[na + nb:, :])
    x1 = _layer_norm(ALPHA * x_ref[...] + gate_ref[...] * y, g_ref[...], bb_ref[...])
    x1_ref[...] = x1
    u2 = _modulate(x1, sc_ref[...], sh_ref[...])
    u2_ref[...] = _bf(u2)
    if route:
        hi = _bf(u2)
        lo = _bf(u2 - hi.astype(F32))
        lg_ref[...] = _dot(hi, rwh_ref[...]) + (_dot(lo, rwh_ref[...]) + _dot(hi, rwl_ref[...]))


def _outproj(x, a, b, c, w, gate, g, bb, sc, sh, rw=None, tm=512):
    s, d = x.shape
    row = lambda i: (i, 0)
    route = rw is not None
    ins = [x, a, b, c, w, gate, g, bb, sc, sh]
    in_specs = [pl.BlockSpec((tm, d), row), pl.BlockSpec((tm, a.shape[1]), row), pl.BlockSpec((tm, b.shape[1]), row),
                pl.BlockSpec((tm, c.shape[1]), row), _full(w.shape)] + [_full((1, d))] * 5
    out_shape = [jax.ShapeDtypeStruct((s, d), F32), jax.ShapeDtypeStruct((s, d), BF16)]
    out_specs = [pl.BlockSpec((tm, d), row), pl.BlockSpec((tm, d), row)]
    if route:
        rwp = jnp.zeros((d, LANES), F32).at[:, :rw.shape[1]].set(rw)
        rwh = _bf(rwp)
        rwl = _bf(rwp - rwh.astype(F32))
        ins += [rwh, rwl]
        in_specs += [_full((d, LANES)), _full((d, LANES))]
        out_shape.append(jax.ShapeDtypeStruct((s, LANES), F32))
        out_specs.append(pl.BlockSpec((tm, LANES), row))
    return pl.pallas_call(
        functools.partial(_outproj_kernel, route=route),
        out_shape=tuple(out_shape),
        grid=(s // tm,),
        in_specs=in_specs,
        out_specs=tuple(out_specs),
        compiler_params=_cp("parallel"),
        name="outproj_ln1",
    )(*ins)


def _ffn_kernel(te_ref, nu_ref, x_ref, wg_ref, wu_ref, wd_ref, y_ref, acc_ref):
    i, j = pl.program_id(0), pl.program_id(1)

    @pl.when(i < nu_ref[0])
    def _():
        @pl.when(j == 0)
        def _():
            acc_ref[...] = jnp.zeros(acc_ref.shape, F32)

        x = x_ref[...]
        gt = _dot(x, wg_ref[0])
        up = _dot(x, wu_ref[0])
        h = _bf(gt * jax.nn.sigmoid(gt) * up)
        acc_ref[...] += _dot(h, wd_ref[0])

        @pl.when(j == pl.num_programs(1) - 1)
        def _():
            y_ref[...] = acc_ref[...]

    @pl.when((i >= nu_ref[0]) & (j == 0))
    def _():
        y_ref[...] = jnp.zeros(y_ref.shape, F32)


def _ffn(xs, wg, wu, wd, tile_expert, n_used, tm=512, tf=512):
    p, d = xs.shape
    f = wg.shape[2]
    nt, nf = p // tm, f // tf

    def clamp(i, j, nu):
        live = i < nu[0]
        return jnp.where(live, i, nu[0] - 1), jnp.where(live, j, nf - 1)

    def xmap(i, j, te, nu):
        return clamp(i, j, nu)[0], 0

    def gmap(i, j, te, nu):
        ii, jj = clamp(i, j, nu)
        return te[ii], 0, jj

    def dmap(i, j, te, nu):
        ii, jj = clamp(i, j, nu)
        return te[ii], jj, 0

    return pl.pallas_call(
        _ffn_kernel,
        out_shape=jax.ShapeDtypeStruct((p, d), F32),
        grid_spec=pltpu.PrefetchScalarGridSpec(
            num_scalar_prefetch=2,
            grid=(nt, nf),
            in_specs=[pl.BlockSpec((tm, d), xmap), pl.BlockSpec((1, d, tf), gmap), pl.BlockSpec((1, d, tf), gmap),
                      pl.BlockSpec((1, tf, d), dmap)],
            out_specs=pl.BlockSpec((tm, d), lambda i, j, te, nu: (i, 0)),
            scratch_shapes=[pltpu.VMEM((tm, d), F32)]),
        compiler_params=_cp("parallel", "arbitrary"),
        name="swiglu_ffn",
    )(tile_expert, n_used, xs, wg, wu, wd)


def _ln2_kernel(x_ref, f_ref, gate_ref, g_ref, b_ref, o_ref):
    o_ref[...] = _layer_norm(ALPHA * x_ref[...] + gate_ref[...] * f_ref[...], g_ref[...], b_ref[...])


def _ln2(x, f, gate, g, b, tm=512):
    s, d = x.shape
    row = lambda i: (i, 0)
    return pl.pallas_call(
        _ln2_kernel,
        out_shape=jax.ShapeDtypeStruct((s, d), F32),
        grid=(s // tm,),
        in_specs=[pl.BlockSpec((tm, d), row), pl.BlockSpec((tm, d), row)] + [_full((1, d))] * 3,
        out_specs=pl.BlockSpec((tm, d), row),
        compiler_params=_cp("parallel"),
        name="residual_ln2",
    )(x, f, gate, g, b)


def _router_kernel(lg_ref, info_ref, cnt_ref, carry_sc):
    i = pl.program_id(0)

    @pl.when(i == 0)
    def _():
        carry_sc[...] = jnp.zeros(carry_sc.shape, F32)

    lg = lg_ref[...]
    tm = lg.shape[0]
    lane = lax.broadcasted_iota(jnp.int32, lg.shape, 1)
    lg = jnp.where(lane < N_EXPERTS, lg, -jnp.inf)
    v1 = jnp.max(lg, axis=-1, keepdims=True)
    i1 = jnp.min(jnp.where(lg == v1, lane, LANES), axis=-1, keepdims=True)
    lg2 = jnp.where(lane == i1, -jnp.inf, lg)
    v2 = jnp.max(lg2, axis=-1, keepdims=True)
    i2 = jnp.min(jnp.where(lg2 == v2, lane, LANES), axis=-1, keepdims=True)
    e21 = jnp.exp(v2 - v1)
    w1 = 1.0 / (1.0 + e21)
    w2 = e21 * w1
    oh1 = lane == i1
    oh2 = lane == i2
    oh = jnp.where(oh1 | oh2, 1.0, 0.0)
    rr = lax.broadcasted_iota(jnp.int32, (tm, tm), 0)
    cc = lax.broadcasted_iota(jnp.int32, (tm, tm), 1)
    strict = jnp.where(cc < rr, 1.0, 0.0).astype(BF16)
    before = _dot(strict, _bf(oh)) + carry_sc[...]
    r1 = jnp.sum(jnp.where(oh1, before, 0.0), axis=-1, keepdims=True)
    r2 = jnp.sum(jnp.where(oh2, before, 0.0), axis=-1, keepdims=True)
    vals = (i1.astype(F32), i2.astype(F32), w1, w2, r1, r2)
    info = jnp.zeros(lg.shape, F32)
    for n, v in enumerate(vals):
        info = jnp.where(lane == n, v, info)
    info_ref[...] = info
    carry_sc[...] += jnp.sum(oh, axis=0, keepdims=True)
    cnt_ref[...] = carry_sc[...]


def _router(logits, tm=512):
    s = logits.shape[0]
    return pl.pallas_call(
        _router_kernel,
        out_shape=(jax.ShapeDtypeStruct((s, LANES), F32), jax.ShapeDtypeStruct((1, LANES), F32)),
        grid=(s // tm,),
        in_specs=[pl.BlockSpec((tm, LANES), lambda i: (i, 0))],
        out_specs=(pl.BlockSpec((tm, LANES), lambda i: (i, 0)), _full((1, LANES))),
        scratch_shapes=[pltpu.VMEM((1, LANES), F32)],
        compiler_params=_cp("arbitrary"),
        name="router_top2",
    )(logits)


def _rope_tables(s):
    rows = s // GRID_W
    row_pos = jnp.broadcast_to(jnp.arange(rows, dtype=F32)[:, None], (rows, GRID_W)).reshape(-1)
    col_pos = jnp.broadcast_to(jnp.arange(GRID_W, dtype=F32)[None, :], (rows, GRID_W)).reshape(-1)
    n_freq = HEAD64 // 4
    inv_freq = ROPE_THETA ** (-jnp.arange(n_freq, dtype=F32) / n_freq)
    ang_r = row_pos[:, None] * inv_freq
    ang_c = col_pos[:, None] * inv_freq
    ang = jnp.concatenate([ang_r, ang_r, ang_c, ang_c] * 2, axis=-1)
    cos, sin = jnp.cos(ang), jnp.sin(ang)
    first_half = (jnp.arange(LANES) % 32) < 16
    return cos, jnp.where(first_half, -sin, 0.0), jnp.where(first_half, 0.0, sin)


def _moe_ffn(u2, logits, wg, wu, wd, tm):
    s, d = u2.shape
    e = wg.shape[0]
    info, counts = _router(logits)
    e1, e2 = info[:, 0].astype(jnp.int32), info[:, 1].astype(jnp.int32)
    w1, w2 = info[:, 2], info[:, 3]
    r1, r2 = info[:, 4].astype(jnp.int32), info[:, 5].astype(jnp.int32)
    cnt = counts[0, :e].astype(jnp.int32)
    tiles = (cnt + tm - 1) // tm
    tile_end = jnp.cumsum(tiles)
    start = (tile_end - tiles) * tm
    nt = (2 * s) // tm + e
    n_used = tile_end[-1:].astype(jnp.int32)
    tile_expert = jnp.minimum(jnp.searchsorted(tile_end, jnp.arange(nt), side="right"), e - 1).astype(jnp.int32)
    d1, d2 = start[e1] + r1, start[e2] + r2
    tok = jnp.arange(s, dtype=jnp.int32)
    src = jnp.zeros((nt * tm,), jnp.int32).at[d1].set(tok).at[d2].set(tok)
    xs = jnp.take(u2, src, axis=0)
    y = _ffn(xs, wg, wu, wd, tile_expert, n_used, tm=tm)
    return w1[:, None] * jnp.take(y, d1, axis=0) + w2[:, None] * jnp.take(y, d2, axis=0)


def _layer(x, c_col, tabs, p, moe):
    s, d = x.shape
    cos, sa, sb = tabs
    mod = _ada(c_col, p["ada_w"], p["ada_b"])
    sh1, sc1, g1, sh2, sc2, g2 = [mod[:, i * d:(i + 1) * d] for i in range(6)]

    w_in = p["w_in"]
    n_gqa = (GQA_HEADS + 2 * GQA_KV) * HEAD64
    n_mla = MLA_QR + MLA_KVR + HEAD64
    w_gqa = _bf(w_in[:, :n_gqa])
    w_mla = _bf(jnp.pad(w_in[:, n_gqa:n_gqa + n_mla], ((0, 0), (0, LANES - HEAD64))))
    w_ml = _bf(jnp.pad(w_in[:, n_gqa + n_mla:], ((0, 0), (0, LANES - 4 * ML_HEADS))))

    ii = jnp.arange(LANES)
    bd = _bf((ii[:, None] // HEAD64 == ii[None, :] // HEAD64).astype(F32))
    qg = jnp.tile(p["gqa_q_norm"], 2).reshape(1, LANES)
    kg = jnp.tile(p["gqa_k_norm"], 2).reshape(1, LANES)
    qT, k, vT = _inproj_gqa(x, sc1, sh1, w_gqa, qg, kg, cos, sa, sb, bd)
    out_a = _attention(qT, k, vT, groups=GQA_HEADS // GQA_KV, dq=HEAD64, dv=HEAD64, cq=128)

    dq = MLA_NOPE + HEAD64
    wuq = p["mla_w_uq"].reshape(MLA_QR, MLA_HEADS, dq)
    wuq = _bf(jnp.concatenate([wuq[:, :, :MLA_NOPE].reshape(MLA_QR, -1), wuq[:, :, MLA_NOPE:].reshape(MLA_QR, -1)], axis=1))
    qTm, km, vTm = _inproj_mla(x, sc1, sh1, w_mla, p["mla_q_norm"].reshape(1, -1), p["mla_kv_norm"].reshape(1, -1),
                               wuq, _bf(p["mla_w_ukv"]), cos, sa, sb)
    out_b = _attention(qTm, km, vTm, groups=1, dq=dq, dv=MLA_V, cq=512)

    gb = jnp.pad(p["mlstm_gate_b"], (0, LANES - 4 * ML_HEADS)).reshape(1, LANES)
    mq, mkT, mv, mo, gT = _inproj_mlstm(x, sc1, sh1, w_ml, p["mlstm_conv"], gb)
    out_c = _mlstm(mq, mkT, mv, mo, gT.reshape(4 * ML_HEADS, s // CHUNK, CHUNK), p["mlstm_head_norm"].reshape(1, -1))

    ln = [p[n].reshape(1, d) for n in ("ln1_g", "ln1_b", "ln2_g", "ln2_b")]
    if moe:
        x1, u2, logits = _outproj(x, out_a, out_b, out_c, _bf(p["w_out"]), g1, ln[0], ln[1], sc2, sh2, rw=p["router_w"])
        f = _moe_ffn(u2, logits, _bf(p["moe_w_gate"]), _bf(p["moe_w_up"]), _bf(p["moe_w_down"]), tm=512)
    else:
        x1, u2 = _outproj(x, out_a, out_b, out_c, _bf(p["w_out"]), g1, ln[0], ln[1], sc2, sh2)
        tm = 512
        nt = s // tm
        f = _ffn(u2, _bf(p["ffn_w_gate"])[None], _bf(p["ffn_w_up"])[None], _bf(p["ffn_w_down"])[None],
                 jnp.zeros((nt,), jnp.int32), jnp.full((1,), nt, jnp.int32), tm=tm)
    return _ln2(x1, f, g2, ln[2], ln[3])


_NAMES = ("ada_w", "ada_b", "w_in", "gqa_q_norm", "gqa_k_norm", "mla_q_norm", "mla_w_uq", "mla_kv_norm", "mla_w_ukv",
          "mlstm_conv", "mlstm_gate_b", "mlstm_head_norm", "w_out", "ln1_g", "ln1_b")


def kernel(x, c, l0_ada_w, l0_ada_b, l0_w_in, l0_gqa_q_norm, l0_gqa_k_norm, l0_mla_q_norm, l0_mla_w_uq, l0_mla_kv_norm, l0_mla_w_ukv, l0_mlstm_conv, l0_mlstm_gate_b, l0_mlstm_head_norm, l0_w_out, l0_ln1_g, l0_ln1_b, l0_ffn_w_gate, l0_ffn_w_up, l0_ffn_w_down, l0_ln2_g, l0_ln2_b, l1_ada_w, l1_ada_b, l1_w_in, l1_gqa_q_norm, l1_gqa_k_norm, l1_mla_q_norm, l1_mla_w_uq, l1_mla_kv_norm, l1_mla_w_ukv, l1_mlstm_conv, l1_mlstm_gate_b, l1_mlstm_head_norm, l1_w_out, l1_ln1_g, l1_ln1_b, l1_router_w, l1_moe_w_gate, l1_moe_w_up, l1_moe_w_down, l1_ln2_g, l1_ln2_b):
    b, s, d = x.shape
    p0 = dict(zip(_NAMES, (l0_ada_w, l0_ada_b, l0_w_in, l0_gqa_q_norm, l0_gqa_k_norm, l0_mla_q_norm, l0_mla_w_uq,
                           l0_mla_kv_norm, l0_mla_w_ukv, l0_mlstm_conv, l0_mlstm_gate_b, l0_mlstm_head_norm, l0_w_out,
                           l0_ln1_g, l0_ln1_b)))
    p0.update(ffn_w_gate=l0_ffn_w_gate, ffn_w_up=l0_ffn_w_up, ffn_w_down=l0_ffn_w_down, ln2_g=l0_ln2_g, ln2_b=l0_ln2_b)
    p1 = dict(zip(_NAMES, (l1_ada_w, l1_ada_b, l1_w_in, l1_gqa_q_norm, l1_gqa_k_norm, l1_mla_q_norm, l1_mla_w_uq,
                           l1_mla_kv_norm, l1_mla_w_ukv, l1_mlstm_conv, l1_mlstm_gate_b, l1_mlstm_head_norm, l1_w_out,
                           l1_ln1_g, l1_ln1_b)))
    p1.update(router_w=l1_router_w, moe_w_gate=l1_moe_w_gate, moe_w_up=l1_moe_w_up, moe_w_down=l1_moe_w_down,
              ln2_g=l1_ln2_g, ln2_b=l1_ln2_b)
    tabs = _rope_tables(s)
    outs = []
    for bi in range(b):
        xb = x[bi]
        c_col = c[bi].reshape(d, 1)
        xb = _layer(xb, c_col, tabs, p0, moe=False)
        xb = _layer(xb, c_col, tabs, p1, moe=True)
        outs.append(xb)
    return jnp.stack(outs, axis=0)
```

```python
import functools

import jax
import jax.numpy as jnp
from jax import lax
from jax.experimental import pallas as pl
from jax.experimental.pallas import tpu as pltpu

F32 = jnp.float32
BF16 = jnp.bfloat16

GRID_W = 64
ROPE_THETA = 10000.0
HEAD64 = 64
GQA_HEADS = 16
GQA_KV = 4
MLA_HEADS = 4
MLA_NOPE = 128
MLA_V = 128
MLA_QR = 512
MLA_KVR = 256
ML_HEADS = 4
ML_D = 128
ML_W = ML_HEADS * ML_D
CHUNK = 128
N_EXPERTS = 8
RMS_EPS = 1e-6
LN_EPS = 1e-5
DEPTH = 2
ALPHA = (2 * DEPTH) ** 0.25

LANES = 128
NEG_BIG = -1e30
LOG2E = 1.4426950408889634
ONES_ROWS = 16
VMEM_LIMIT = 56 * 1024 * 1024
FFN_TM = 1024


def _cp(*sem, vmem=VMEM_LIMIT):
    return pltpu.CompilerParams(dimension_semantics=sem, vmem_limit_bytes=vmem)


def _full(shape):
    n = len(shape)
    return pl.BlockSpec(shape, lambda *_: (0,) * n)


def _bf(x):
    return x.astype(BF16)


def _dot(a, b):
    return jnp.dot(a, b, preferred_element_type=F32)


def _split3(x):
    hi = _bf(x)
    r1 = x - hi.astype(F32)
    mid = _bf(r1)
    lo = _bf(r1 - mid.astype(F32))
    return hi, mid, lo


def _ada_kernel(c_ref, w_ref, b_ref, o_ref):
    c = c_ref[...]
    ca = c * jax.nn.sigmoid(c)
    o_ref[...] = jnp.sum(ca * w_ref[...], axis=0, keepdims=True) + b_ref[...]


def _ada(c_col, w, b):
    d, n = w.shape
    tn = 1024
    return pl.pallas_call(
        _ada_kernel,
        out_shape=jax.ShapeDtypeStruct((1, n), F32),
        grid=(n // tn,),
        in_specs=[_full((d, 1)), pl.BlockSpec((d, tn), lambda j: (0, j)), pl.BlockSpec((1, tn), lambda j: (0, j))],
        out_specs=pl.BlockSpec((1, tn), lambda j: (0, j)),
        compiler_params=_cp("parallel"),
        name="ada_mod",
    )(c_col, w, b.reshape(1, n))


def _group_sumsq(blk, bd):
    hi, mid, lo = _split3(blk * blk)
    return _dot(hi, bd) + _dot(mid, bd) + _dot(lo, bd)


def _rope(blk, cos, sin_a, sin_b):
    return blk * cos + pltpu.roll(blk, LANES - 16, 1) * sin_a + pltpu.roll(blk, 16, 1) * sin_b


def _modulate(x, scale, shift):
    return x * (1.0 + scale) + shift


def _ones_rows(n):
    r = lax.broadcasted_iota(jnp.int32, (ONES_ROWS, n), 0)
    return jnp.where(r == 0, 1.0, 0.0).astype(BF16)


def _inproj_gqa_kernel(x_ref, sc_ref, sh_ref, w_ref, qg_ref, kg_ref, cos_ref, sa_ref, sb_ref, bd_ref,
                       qT_ref, k_ref, vT_ref):
    u = _bf(_modulate(x_ref[...], sc_ref[...], sh_ref[...]))
    y = _dot(u, w_ref[...])
    cos, sa, sb, bd = cos_ref[...], sa_ref[...], sb_ref[...], bd_ref[...]
    nq = GQA_HEADS * HEAD64 // LANES
    nk = GQA_KV * HEAD64 // LANES
    scale = HEAD64 ** -0.5 * LOG2E
    for j in range(nq):
        blk = y[:, j * LANES:(j + 1) * LANES]
        ss = _group_sumsq(blk, bd)
        blk = blk * lax.rsqrt(ss * (1.0 / HEAD64) + RMS_EPS) * qg_ref[...]
        blk = _rope(blk, cos, sa, sb) * scale
        qT_ref[j * LANES:(j + 1) * LANES, :] = _bf(blk.T)
    for j in range(nk):
        blk = y[:, (nq + j) * LANES:(nq + j + 1) * LANES]
        ss = _group_sumsq(blk, bd)
        blk = blk * lax.rsqrt(ss * (1.0 / HEAD64) + RMS_EPS) * kg_ref[...]
        blk = _bf(_rope(blk, cos, sa, sb))
        k_ref[2 * j] = blk[:, :HEAD64]
        k_ref[2 * j + 1] = blk[:, HEAD64:]
    vT = _bf(y[:, (nq + nk) * LANES:].T)
    ones = _ones_rows(vT.shape[1])
    dve = HEAD64 + ONES_ROWS
    for h in range(GQA_KV):
        vT_ref[h * dve:h * dve + HEAD64, :] = vT[h * HEAD64:(h + 1) * HEAD64, :]
        vT_ref[h * dve + HEAD64:(h + 1) * dve, :] = ones


def _inproj_gqa(x, sc, sh, w, qg, kg, cos, sa, sb, bd, tm=256):
    s, d = x.shape
    n = w.shape[1]
    row = lambda i: (i, 0)
    return pl.pallas_call(
        _inproj_gqa_kernel,
        out_shape=(jax.ShapeDtypeStruct((GQA_HEADS * HEAD64, s), BF16),
                   jax.ShapeDtypeStruct((GQA_KV, s, HEAD64), BF16),
                   jax.ShapeDtypeStruct((GQA_KV * (HEAD64 + ONES_ROWS), s), BF16)),
        grid=(s // tm,),
        in_specs=[pl.BlockSpec((tm, d), row), _full((1, d)), _full((1, d)), _full((d, n)),
                  _full((1, LANES)), _full((1, LANES)),
                  pl.BlockSpec((tm, LANES), row), pl.BlockSpec((tm, LANES), row), pl.BlockSpec((tm, LANES), row),
                  _full((LANES, LANES))],
        out_specs=(pl.BlockSpec((GQA_HEADS * HEAD64, tm), lambda i: (0, i)),
                   pl.BlockSpec((GQA_KV, tm, HEAD64), lambda i: (0, i, 0)),
                   pl.BlockSpec((GQA_KV * (HEAD64 + ONES_ROWS), tm), lambda i: (0, i))),
        compiler_params=_cp("parallel"),
        name="inproj_gqa",
    )(x, sc, sh, w, qg, kg, cos, sa, sb, bd)


def _inproj_mla_kernel(x_ref, sc_ref, sh_ref, w_ref, qn_ref, kvn_ref, wuq_ref, wukv_ref, cos_ref, sa_ref, sb_ref,
                       qT_ref, k_ref, vT_ref):
    u = _bf(_modulate(x_ref[...], sc_ref[...], sh_ref[...]))
    y = _dot(u, w_ref[...])
    cos, sa, sb = cos_ref[...], sa_ref[...], sb_ref[...]
    cq = y[:, :MLA_QR]
    ckv = y[:, MLA_QR:MLA_QR + MLA_KVR]
    kr = y[:, MLA_QR + MLA_KVR:]
    cqn = cq * lax.rsqrt(jnp.mean(cq * cq, axis=-1, keepdims=True) + RMS_EPS) * qn_ref[...]
    ckvn = ckv * lax.rsqrt(jnp.mean(ckv * ckv, axis=-1, keepdims=True) + RMS_EPS) * kvn_ref[...]
    qm = _dot(_bf(cqn), wuq_ref[...])
    kvm = _dot(_bf(ckvn), wukv_ref[...])
    scale = (MLA_NOPE + HEAD64) ** -0.5 * LOG2E
    dq = MLA_NOPE + HEAD64
    dve = MLA_V + ONES_ROWS
    ones = _ones_rows(y.shape[0])
    k_rope = _bf(_rope(kr, cos, sa, sb)[:, :HEAD64])
    for p in range(MLA_HEADS // 2):
        rblk = qm[:, MLA_HEADS * MLA_NOPE + p * LANES:MLA_HEADS * MLA_NOPE + (p + 1) * LANES]
        rT = _bf((_rope(rblk, cos, sa, sb) * scale).T)
        for e in range(2):
            h = 2 * p + e
            qT_ref[h * dq + MLA_NOPE:(h + 1) * dq, :] = rT[e * HEAD64:(e + 1) * HEAD64, :]
    for h in range(MLA_HEADS):
        nope = qm[:, h * MLA_NOPE:(h + 1) * MLA_NOPE] * scale
        qT_ref[h * dq:h * dq + MLA_NOPE, :] = _bf(nope.T)
        k_ref[h, :, :MLA_NOPE] = _bf(kvm[:, h * 2 * LANES:h * 2 * LANES + MLA_NOPE])
        k_ref[h, :, MLA_NOPE:] = k_rope
        vT_ref[h * dve:h * dve + MLA_V, :] = _bf(kvm[:, h * 2 * LANES + MLA_NOPE:(h + 1) * 2 * LANES].T)
        vT_ref[h * dve + MLA_V:(h + 1) * dve, :] = ones


def _inproj_mla(x, sc, sh, w, qn, kvn, wuq, wukv, cos, sa, sb, tm=256):
    s, d = x.shape
    n = w.shape[1]
    dq = MLA_NOPE + HEAD64
    row = lambda i: (i, 0)
    return pl.pallas_call(
        _inproj_mla_kernel,
        out_shape=(jax.ShapeDtypeStruct((MLA_HEADS * dq, s), BF16),
                   jax.ShapeDtypeStruct((MLA_HEADS, s, dq), BF16),
                   jax.ShapeDtypeStruct((MLA_HEADS * (MLA_V + ONES_ROWS), s), BF16)),
        grid=(s // tm,),
        in_specs=[pl.BlockSpec((tm, d), row), _full((1, d)), _full((1, d)), _full((d, n)),
                  _full((1, MLA_QR)), _full((1, MLA_KVR)), _full(wuq.shape), _full(wukv.shape),
                  pl.BlockSpec((tm, LANES), row), pl.BlockSpec((tm, LANES), row), pl.BlockSpec((tm, LANES), row)],
        out_specs=(pl.BlockSpec((MLA_HEADS * dq, tm), lambda i: (0, i)),
                   pl.BlockSpec((MLA_HEADS, tm, dq), lambda i: (0, i, 0)),
                   pl.BlockSpec((MLA_HEADS * (MLA_V + ONES_ROWS), tm), lambda i: (0, i))),
        compiler_params=_cp("parallel"),
        name="inproj_mla",
    )(x, sc, sh, w, qn, kvn, wuq, wukv, cos, sa, sb)


def _inproj_mlstm_kernel(x_ref, xp_ref, xn_ref, sc_ref, sh_ref, w_ref, conv_ref, gb_ref,
                         q_ref, kT_ref, v_ref, o_ref, gT_ref):
    i = pl.program_id(0)
    last = pl.num_programs(0) - 1
    sc, sh = sc_ref[...], sh_ref[...]
    u = _bf(_modulate(x_ref[...], sc, sh))
    y = _dot(u, w_ref[...])
    tm = y.shape[0]
    qk = y[:, :2 * ML_W]
    halo = jnp.concatenate([xp_ref[...], xn_ref[...]], axis=0)
    hy = _dot(_bf(_modulate(halo, sc, sh)), w_ref[:, :2 * ML_W])
    prev_row = jnp.where(i == 0, 0.0, hy[7:8, :])
    next_row = jnp.where(i == last, 0.0, hy[8:9, :])
    rows = lax.broadcasted_iota(jnp.int32, qk.shape, 0)
    prev = jnp.where(rows == 0, prev_row, pltpu.roll(qk, 1, 0))
    nxt = jnp.where(rows == tm - 1, next_row, pltpu.roll(qk, tm - 1, 0))
    cw = conv_ref[...]
    z = cw[0:1, :] * prev + cw[1:2, :] * qk + cw[2:3, :] * nxt
    z = z * jax.nn.sigmoid(z)
    q_ref[...] = _bf(z[:, :ML_W])
    kT_ref[...] = (z[:, ML_W:] * (ML_D ** -0.5)).T
    v_ref[...] = _bf(y[:, 2 * ML_W:3 * ML_W])
    o_ref[...] = y[:, 3 * ML_W:4 * ML_W]
    g = y[:, 4 * ML_W:] + gb_ref[...]
    gT_ref[...] = g.T[:4 * ML_HEADS, :]


def _inproj_mlstm(x, sc, sh, w, conv, gb, tm=256):
    s, d = x.shape
    n = w.shape[1]
    row = lambda i: (i, 0)
    col = lambda i: (0, i)
    r8 = tm // 8
    return pl.pallas_call(
        _inproj_mlstm_kernel,
        out_shape=(jax.ShapeDtypeStruct((s, ML_W), BF16), jax.ShapeDtypeStruct((ML_W, s), F32),
                   jax.ShapeDtypeStruct((s, ML_W), BF16), jax.ShapeDtypeStruct((s, ML_W), F32),
                   jax.ShapeDtypeStruct((4 * ML_HEADS, s), F32)),
        grid=(s // tm,),
        in_specs=[pl.BlockSpec((tm, d), row),
                  pl.BlockSpec((8, d), lambda i: (jnp.maximum(i * r8 - 1, 0), 0)),
                  pl.BlockSpec((8, d), lambda i: (jnp.minimum((i + 1) * r8, s // 8 - 1), 0)),
                  _full((1, d)), _full((1, d)), _full((d, n)), _full((3, 2 * ML_W)), _full((1, LANES))],
        out_specs=(pl.BlockSpec((tm, ML_W), row), pl.BlockSpec((ML_W, tm), col),
                   pl.BlockSpec((tm, ML_W), row), pl.BlockSpec((tm, ML_W), row),
                   pl.BlockSpec((4 * ML_HEADS, tm), col)),
        compiler_params=_cp("parallel"),
        name="inproj_mlstm",
    )(x, x, x, sc, sh, w, conv, gb)


def _attn_kernel(qT_ref, k_ref, vT_ref, o_ref, s0_sc, s1_sc, acc_sc, *, groups, dq, dv, tk, unroll):
    seq = k_ref.shape[1]
    n = seq // tk
    if groups == 1:
        qT = qT_ref[...]
    else:
        qT = jnp.concatenate([qT_ref[g * dq:(g + 1) * dq, :] for g in range(groups)], axis=1)
    acc_sc[...] = jnp.zeros(acc_sc.shape, F32)

    def scores(c, s_sc):
        off = pl.multiple_of(c * tk, tk)
        s = _dot(k_ref[0, pl.ds(off, tk), :], qT)
        s_sc[...] = s
        return jnp.max(s, axis=0, keepdims=True)

    def accumulate(c, s_sc, mx, m_old):
        off = pl.multiple_of(c * tk, tk)
        m_new = jnp.maximum(m_old, mx)
        p = jnp.exp2(s_sc[...] - m_new)
        acc_sc[...] = jnp.exp2(m_old - m_new) * acc_sc[...] + _dot(vT_ref[:, pl.ds(off, tk)], _bf(p))
        return m_new

    def body(i, carry):
        mx, m = carry
        bufs = (s0_sc, s1_sc)
        for j in range(unroll):
            c = unroll * i + j
            mx_next = scores(jnp.minimum(c + 1, n - 1), bufs[(j + 1) % 2])
            m = accumulate(c, bufs[j % 2], mx, m)
            mx = mx_next
        return mx, m

    nq = acc_sc.shape[1]
    lax.fori_loop(0, n // unroll, body, (scores(0, s0_sc), jnp.full((1, nq), NEG_BIG, F32)))
    oT = acc_sc[:dv, :] * (1.0 / acc_sc[dv:dv + 1, :])
    c = qT_ref.shape[1]
    if groups > 1:
        oT = jnp.concatenate([oT[:, g * c:(g + 1) * c] for g in range(groups)], axis=0)
    o_ref[...] = _bf(oT.T)


def _attention(qT, k, vT, *, groups, dq, dv, cq, tk=512, unroll=4):
    h, s, _ = k.shape
    nq = groups * cq
    dve = dv + ONES_ROWS
    unroll = min(unroll, s // tk)
    assert unroll % 2 == 0 and (s // tk) % unroll == 0
    kern = functools.partial(_attn_kernel, groups=groups, dq=dq, dv=dv, tk=tk, unroll=unroll)
    return pl.pallas_call(
        kern,
        out_shape=jax.ShapeDtypeStruct((s, h * groups * dv), BF16),
        grid=(h, s // cq),
        in_specs=[pl.BlockSpec((groups * dq, cq), lambda a, i: (a, i)),
                  pl.BlockSpec((1, s, dq), lambda a, i: (a, 0, 0)),
                  pl.BlockSpec((dve, s), lambda a, i: (a, 0))],
        out_specs=pl.BlockSpec((cq, groups * dv), lambda a, i: (i, a)),
        scratch_shapes=[pltpu.VMEM((tk, nq), F32), pltpu.VMEM((tk, nq), F32), pltpu.VMEM((dve, nq), F32)],
        compiler_params=_cp("parallel", "parallel"),
        name="attention_g%d" % groups,
    )(qT, k, vT)


def _mlstm_kernel(q_ref, kT_ref, v_ref, o_ref, g_ref, hn_ref, out_ref,
                  cn_sc, mprev_sc, w_sc, r_sc, lf_sc, g_sc, mloc_sc, hacc_sc):
    head = pl.program_id(0)
    nc = g_ref.shape[1]
    L = CHUNK
    ii = lax.broadcasted_iota(jnp.int32, (L, L), 0)
    jj = lax.broadcasted_iota(jnp.int32, (L, L), 1)
    e0 = jnp.where(lax.broadcasted_iota(jnp.int32, (L, LANES), 1) == 0, 1.0, 0.0).astype(BF16)

    for direction in range(2):
        fw = direction == 0
        mask = (jj <= ii) if fw else (jj >= ii)
        tri = jnp.where((ii <= jj) if fw else (ii >= jj), 1.0, 0.0).astype(BF16)
        li = g_ref[2 * ML_HEADS * direction + head]
        fg = g_ref[2 * ML_HEADS * direction + ML_HEADS + head]
        lf = jnp.minimum(fg, 0.0) - jnp.log1p(jnp.exp(-jnp.abs(fg)))
        hi, mid, lo = _split3(lf)
        b = _dot(hi, tri) + _dot(mid, tri) + _dot(lo, tri)
        gtot = jnp.sum(lf, axis=-1, keepdims=True)
        a = gtot - b + li
        mloc = jnp.max(a, axis=-1, keepdims=True)
        w_sc[...] = jnp.exp(a - mloc)
        r_sc[...] = li - b
        lf_sc[...] = lf
        g_sc[...] = jnp.broadcast_to(gtot, (nc, LANES))
        mloc_sc[...] = jnp.broadcast_to(mloc, (nc, LANES))

        def scan_body(i, carry, fw=fw):
            cn, m = carry
            cc = i if fw else nc - 1 - i
            off = pl.multiple_of(cc * L, L)
            cn_sc[cc] = _bf(cn)
            mprev_sc[pl.ds(cc, 1), :] = jnp.broadcast_to(m, (1, LANES))
            kw = _bf(kT_ref[:, pl.ds(off, L)] * w_sc[pl.ds(cc, 1), :])
            vext = jnp.concatenate([v_ref[pl.ds(off, L), :], e0], axis=1)
            kv = _dot(kw, vext)
            g = g_sc[pl.ds(cc, 1), 0:1]
            ml = mloc_sc[pl.ds(cc, 1), 0:1]
            m_new = jnp.maximum(g + m, ml)
            cn = jnp.exp(g + m - m_new) * cn + jnp.exp(ml - m_new) * kv
            return cn, m_new

        lax.fori_loop(0, nc, scan_body, (jnp.zeros((ML_D, 2 * ML_D), F32), jnp.zeros((1, 1), F32)))

        def out_body(cc, carry, fw=fw, mask=mask):
            off = pl.multiple_of(cc * L, L)
            mp = mprev_sc[pl.ds(cc, 1), 0:1]
            rm = jnp.where(mask, r_sc[pl.ds(cc, 1), :], -jnp.inf)
            mt = jnp.maximum(mp, jnp.max(rm, axis=-1, keepdims=True))
            d = jnp.exp(rm - mt)
            iw = jnp.exp(mp - mt)
            qc = q_ref[pl.ds(off, L), :]
            qk = _dot(qc, _bf(kT_ref[:, pl.ds(off, L)])) * d
            num_i = _dot(_bf(qk), v_ref[pl.ds(off, L), :])
            den_i = jnp.sum(qk, axis=-1, keepdims=True)
            inter = _dot(qc, cn_sc[cc])
            num = num_i + iw * inter[:, :ML_D]
            den = den_i + iw * inter[:, ML_D:ML_D + 1]
            bcol = jnp.sum(jnp.where(mask, lf_sc[pl.ds(cc, 1), :], 0.0), axis=-1, keepdims=True)
            h = num / jnp.maximum(jnp.abs(den), jnp.exp(-(bcol + mt)))
            if fw:
                hacc_sc[pl.ds(off, L), :] = h
            else:
                h = h + hacc_sc[pl.ds(off, L), :]
                mu = jnp.mean(h, axis=-1, keepdims=True)
                hc = h - mu
                var = jnp.mean(hc * hc, axis=-1, keepdims=True)
                hn = hc * lax.rsqrt(var + LN_EPS) * hn_ref[...]
                out_ref[pl.ds(off, L), :] = _bf(jax.nn.sigmoid(o_ref[pl.ds(off, L), :]) * hn)
            return carry

        lax.fori_loop(0, nc, out_body, 0)


def _mlstm(q, kT, v, o, g3, hn):
    s = q.shape[0]
    nc = s // CHUNK
    colblk = lambda h: (0, h)
    return pl.pallas_call(
        _mlstm_kernel,
        out_shape=jax.ShapeDtypeStruct((s, ML_W), BF16),
        grid=(ML_HEADS,),
        in_specs=[pl.BlockSpec((s, ML_D), colblk), pl.BlockSpec((ML_D, s), lambda h: (h, 0)),
                  pl.BlockSpec((s, ML_D), colblk), pl.BlockSpec((s, ML_D), colblk),
                  _full(g3.shape), pl.BlockSpec((1, ML_D), colblk)],
        out_specs=pl.BlockSpec((s, ML_D), colblk),
        scratch_shapes=[pltpu.VMEM((nc, ML_D, 2 * ML_D), BF16), pltpu.VMEM((nc, LANES), F32),
                        pltpu.VMEM((nc, CHUNK), F32), pltpu.VMEM((nc, CHUNK), F32), pltpu.VMEM((nc, CHUNK), F32),
                        pltpu.VMEM((nc, LANES), F32), pltpu.VMEM((nc, LANES), F32),
                        pltpu.VMEM((s, ML_D), F32)],
        compiler_params=_cp("parallel"),
        name="mlstm",
    )(q, kT, v, o, g3, hn)


def _layer_norm(z, g, b):
    mu = jnp.mean(z, axis=-1, keepdims=True)
    zc = z - mu
    var = jnp.mean(zc * zc, axis=-1, keepdims=True)
    return zc * lax.rsqrt(var + LN_EPS) * g + b


def _outproj_kernel(*refs, route):
    if route:
        (x_ref, a_ref, b_ref, c_ref, w_ref, gate_ref, g_ref, bb_ref, sc_ref, sh_ref, rwh_ref, rwl_ref,
         x1_ref, u2_ref, lg_ref) = refs
    else:
        x_ref, a_ref, b_ref, c_ref, w_ref, gate_ref, g_ref, bb_ref, sc_ref, sh_ref, x1_ref, u2_ref = refs
    na, nb = a_ref.shape[1], b_ref.shape[1]
    y = _dot(a_ref[...], w_ref[:na, :]) + _dot(b_ref[...], w_ref[na:na + nb, :]) + _dot(c_ref[...], w_ref[na + nb:, :])
    x1 = _layer_norm(ALPHA * x_ref[...] + gate_ref[...] * y, g_ref[...], bb_ref[...])
    x1_ref[...] = x1
    u2 = _modulate(x1, sc_ref[...], sh_ref[...])
    u2_ref[...] = _bf(u2)
    if route:
        hi = _bf(u2)
        lo = _bf(u2 - hi.astype(F32))
        lg_ref[...] = _dot(hi, rwh_ref[...]) + (_dot(lo, rwh_ref[...]) + _dot(hi, rwl_ref[...]))


def _outproj(x, a, b, c, w, gate, g, bb, sc, sh, rw=None, tm=512):
    s, d = x.shape
    row = lambda i: (i, 0)
    route = rw is not None
    ins = [x, a, b, c, w, gate, g, bb, sc, sh]
    in_specs = [pl.BlockSpec((tm, d), row), pl.BlockSpec((tm, a.shape[1]), row), pl.BlockSpec((tm, b.shape[1]), row),
                pl.BlockSpec((tm, c.shape[1]), row), _full(w.shape)] + [_full((1, d))] * 5
    out_shape = [jax.ShapeDtypeStruct((s, d), F32), jax.ShapeDtypeStruct((s, d), BF16)]
    out_specs = [pl.BlockSpec((tm, d), row), pl.BlockSpec((tm, d), row)]
    if route:
        rwp = jnp.zeros((d, LANES), F32).at[:, :rw.shape[1]].set(rw)
        rwh = _bf(rwp)
        rwl = _bf(rwp - rwh.astype(F32))
        ins += [rwh, rwl]
        in_specs += [_full((d, LANES)), _full((d, LANES))]
        out_shape.append(jax.ShapeDtypeStruct((s, LANES), F32))
        out_specs.append(pl.BlockSpec((tm, LANES), row))
    return pl.pallas_call(
        functools.partial(_outproj_kernel, route=route),
        out_shape=tuple(out_shape),
        grid=(s // tm,),
        in_specs=in_specs,
        out_specs=tuple(out_specs),
        compiler_params=_cp("parallel"),
        name="outproj_ln1",
    )(*ins)


def _ffn_kernel(te_ref, nu_ref, x_ref, wg_ref, wu_ref, wd_ref, y_ref):
    i, j = pl.program_id(0), pl.program_id(1)
    live = i < nu_ref[0]

    def partial_out():
        x = x_ref[...]
        gt = _dot(x, _bf(wg_ref[0]))
        up = _dot(x, _bf(wu_ref[0]))
        h = _bf(gt * jax.nn.sigmoid(gt) * up)
        return _dot(h, _bf(wd_ref[0]))

    @pl.when(live & (j == 0))
    def _():
        y_ref[...] = partial_out()

    @pl.when(live & (j > 0))
    def _():
        y_ref[...] += partial_out()

    @pl.when(jnp.logical_not(live) & (j == 0))
    def _():
        y_ref[...] = jnp.zeros(y_ref.shape, F32)


def _ffn(xs, wg, wu, wd, tile_expert, n_used, tm, tf=256):
    p, d = xs.shape
    f = wg.shape[2]
    nt, nf = p // tm, f // tf

    def clamp(i, j, nu):
        live = i < nu[0]
        return jnp.where(live, i, nu[0] - 1), jnp.where(live, j, nf - 1)

    def xmap(i, j, te, nu):
        return clamp(i, j, nu)[0], 0

    def gmap(i, j, te, nu):
        ii, jj = clamp(i, j, nu)
        return te[ii], 0, jj

    def dmap(i, j, te, nu):
        ii, jj = clamp(i, j, nu)
        return te[ii], jj, 0

    return pl.pallas_call(
        _ffn_kernel,
        out_shape=jax.ShapeDtypeStruct((p, d), F32),
        grid_spec=pltpu.PrefetchScalarGridSpec(
            num_scalar_prefetch=2,
            grid=(nt, nf),
            in_specs=[pl.BlockSpec((tm, d), xmap), pl.BlockSpec((1, d, tf), gmap), pl.BlockSpec((1, d, tf), gmap),
                      pl.BlockSpec((1, tf, d), dmap)],
            out_specs=pl.BlockSpec((tm, d), lambda i, j, te, nu: (i, 0))),
        compiler_params=_cp("parallel", "arbitrary"),
        name="swiglu_ffn",
    )(tile_expert, n_used, xs, wg, wu, wd)


def _ln2_kernel(x_ref, f_ref, gate_ref, g_ref, b_ref, o_ref):
    o_ref[...] = _layer_norm(ALPHA * x_ref[...] + gate_ref[...] * f_ref[...], g_ref[...], b_ref[...])


def _ln2(x, f, gate, g, b, tm=512):
    s, d = x.shape
    row = lambda i: (i, 0)
    return pl.pallas_call(
        _ln2_kernel,
        out_shape=jax.ShapeDtypeStruct((s, d), F32),
        grid=(s // tm,),
        in_specs=[pl.BlockSpec((tm, d), row), pl.BlockSpec((tm, d), row)] + [_full((1, d))] * 3,
        out_specs=pl.BlockSpec((tm, d), row),
        compiler_params=_cp("parallel"),
        name="residual_ln2",
    )(x, f, gate, g, b)


def _router_kernel(lg_ref, info_ref, cnt_ref, carry_sc):
    i = pl.program_id(0)

    @pl.when(i == 0)
    def _():
        carry_sc[...] = jnp.zeros(carry_sc.shape, F32)

    lg = lg_ref[...]
    tm = lg.shape[0]
    lane = lax.broadcasted_iota(jnp.int32, lg.shape, 1)
    lg = jnp.where(lane < N_EXPERTS, lg, -jnp.inf)
    v1 = jnp.max(lg, axis=-1, keepdims=True)
    i1 = jnp.min(jnp.where(lg == v1, lane, LANES), axis=-1, keepdims=True)
    lg2 = jnp.where(lane == i1, -jnp.inf, lg)
    v2 = jnp.max(lg2, axis=-1, keepdims=True)
    i2 = jnp.min(jnp.where(lg2 == v2, lane, LANES), axis=-1, keepdims=True)
    e21 = jnp.exp(v2 - v1)
    w1 = 1.0 / (1.0 + e21)
    w2 = e21 * w1
    oh1 = lane == i1
    oh2 = lane == i2
    oh = jnp.where(oh1 | oh2, 1.0, 0.0)
    rr = lax.broadcasted_iota(jnp.int32, (tm, tm), 0)
    cc = lax.broadcasted_iota(jnp.int32, (tm, tm), 1)
    strict = jnp.where(cc < rr, 1.0, 0.0).astype(BF16)
    before = _dot(strict, _bf(oh)) + carry_sc[...]
    r1 = jnp.sum(jnp.where(oh1, before, 0.0), axis=-1, keepdims=True)
    r2 = jnp.sum(jnp.where(oh2, before, 0.0), axis=-1, keepdims=True)
    vals = (i1.astype(F32), i2.astype(F32), w1, w2, r1, r2)
    info = jnp.zeros(lg.shape, F32)
    for n, v in enumerate(vals):
        info = jnp.where(lane == n, v, info)
    info_ref[...] = info
    carry_sc[...] += jnp.sum(oh, axis=0, keepdims=True)
    cnt_ref[...] = carry_sc[...]


def _router(logits, tm=512):
    s = logits.shape[0]
    return pl.pallas_call(
        _router_kernel,
        out_shape=(jax.ShapeDtypeStruct((s, LANES), F32), jax.ShapeDtypeStruct((1, LANES), F32)),
        grid=(s // tm,),
        in_specs=[pl.BlockSpec((tm, LANES), lambda i: (i, 0))],
        out_specs=(pl.BlockSpec((tm, LANES), lambda i: (i, 0)), _full((1, LANES))),
        scratch_shapes=[pltpu.VMEM((1, LANES), F32)],
        compiler_params=_cp("arbitrary"),
        name="router_top2",
    )(logits)


def _rope_tables(s):
    rows = s // GRID_W
    row_pos = jnp.broadcast_to(jnp.arange(rows, dtype=F32)[:, None], (rows, GRID_W)).reshape(-1)
    col_pos = jnp.broadcast_to(jnp.arange(GRID_W, dtype=F32)[None, :], (rows, GRID_W)).reshape(-1)
    n_freq = HEAD64 // 4
    inv_freq = ROPE_THETA ** (-jnp.arange(n_freq, dtype=F32) / n_freq)
    ang_r = row_pos[:, None] * inv_freq
    ang_c = col_pos[:, None] * inv_freq
    ang = jnp.concatenate([ang_r, ang_r, ang_c, ang_c] * 2, axis=-1)
    cos, sin = jnp.cos(ang), jnp.sin(ang)
    first_half = (jnp.arange(LANES) % 32) < 16
    return cos, jnp.where(first_half, -sin, 0.0), jnp.where(first_half, 0.0, sin)


def _moe_ffn(u2, logits, wg, wu, wd, tm):
    s, d = u2.shape
    e = wg.shape[0]
    info, counts = _router(logits)
    e1, e2 = info[:, 0].astype(jnp.int32), info[:, 1].astype(jnp.int32)
    w1, w2 = info[:, 2], info[:, 3]
    r1, r2 = info[:, 4].astype(jnp.int32), info[:, 5].astype(jnp.int32)
    cnt = counts[0, :e].astype(jnp.int32)
    tiles = (cnt + tm - 1) // tm
    tile_end = jnp.cumsum(tiles)
    start = (tile_end - tiles) * tm
    nt = (2 * s) // tm + e
    n_used = tile_end[-1:].astype(jnp.int32)
    tile_expert = jnp.sum((jnp.arange(nt)[:, None] >= tile_end[None, :]).astype(jnp.int32), axis=1)
    tile_expert = jnp.minimum(tile_expert, e - 1)
    d1, d2 = start[e1] + r1, start[e2] + r2
    tok = jnp.arange(s, dtype=jnp.int32)
    src = jnp.zeros((nt * tm,), jnp.int32).at[d1].set(tok).at[d2].set(tok)
    xs = jnp.take(u2, src, axis=0)
    y = _ffn(xs, wg, wu, wd, tile_expert, n_used, tm=tm)
    return w1[:, None] * jnp.take(y, d1, axis=0) + w2[:, None] * jnp.take(y, d2, axis=0)


def _layer(x, c_col, tabs, p, moe):
    s, d = x.shape
    cos, sa, sb = tabs
    mod = _ada(c_col, p["ada_w"], p["ada_b"])
    sh1, sc1, g1, sh2, sc2, g2 = [mod[:, i * d:(i + 1) * d] for i in range(6)]

    w_in = p["w_in"]
    n_gqa = (GQA_HEADS + 2 * GQA_KV) * HEAD64
    n_mla = MLA_QR + MLA_KVR + HEAD64
    w_gqa = _bf(w_in[:, :n_gqa])
    w_mla = _bf(jnp.pad(w_in[:, n_gqa:n_gqa + n_mla], ((0, 0), (0, LANES - HEAD64))))
    w_ml = _bf(jnp.pad(w_in[:, n_gqa + n_mla:], ((0, 0), (0, LANES - 4 * ML_HEADS))))

    ii = jnp.arange(LANES)
    bd = _bf((ii[:, None] // HEAD64 == ii[None, :] // HEAD64).astype(F32))
    qg = jnp.tile(p["gqa_q_norm"], 2).reshape(1, LANES)
    kg = jnp.tile(p["gqa_k_norm"], 2).reshape(1, LANES)
    qT, k, vT = _inproj_gqa(x, sc1, sh1, w_gqa, qg, kg, cos, sa, sb, bd)
    out_a = _attention(qT, k, vT, groups=GQA_HEADS // GQA_KV, dq=HEAD64, dv=HEAD64, cq=128)

    dq = MLA_NOPE + HEAD64
    wuq = p["mla_w_uq"].reshape(MLA_QR, MLA_HEADS, dq)
    wuq = _bf(jnp.concatenate([wuq[:, :, :MLA_NOPE].reshape(MLA_QR, -1), wuq[:, :, MLA_NOPE:].reshape(MLA_QR, -1)], axis=1))
    qTm, km, vTm = _inproj_mla(x, sc1, sh1, w_mla, p["mla_q_norm"].reshape(1, -1), p["mla_kv_norm"].reshape(1, -1),
                               wuq, _bf(p["mla_w_ukv"]), cos, sa, sb)
    out_b = _attention(qTm, km, vTm, groups=1, dq=dq, dv=MLA_V, cq=512)

    gb = jnp.pad(p["mlstm_gate_b"], (0, LANES - 4 * ML_HEADS)).reshape(1, LANES)
    mq, mkT, mv, mo, gT = _inproj_mlstm(x, sc1, sh1, w_ml, p["mlstm_conv"], gb)
    out_c = _mlstm(mq, mkT, mv, mo, gT.reshape(4 * ML_HEADS, s // CHUNK, CHUNK), p["mlstm_head_norm"].reshape(1, -1))

    ln = [p[n].reshape(1, d) for n in ("ln1_g", "ln1_b", "ln2_g", "ln2_b")]
    if moe:
        x1, u2, logits = _outproj(x, out_a, out_b, out_c, _bf(p["w_out"]), g1, ln[0], ln[1], sc2, sh2, rw=p["router_w"])
        f = _moe_ffn(u2, logits, p["moe_w_gate"], p["moe_w_up"], p["moe_w_down"], tm=FFN_TM)
    else:
        x1, u2 = _outproj(x, out_a, out_b, out_c, _bf(p["w_out"]), g1, ln[0], ln[1], sc2, sh2)
        nt = s // FFN_TM
        f = _ffn(u2, p["ffn_w_gate"][None], p["ffn_w_up"][None], p["ffn_w_down"][None],
                 jnp.zeros((nt,), jnp.int32), jnp.full((1,), nt, jnp.int32), tm=FFN_TM)
    return _ln2(x1, f, g2, ln[2], ln[3])


_NAMES = ("ada_w", "ada_b", "w_in", "gqa_q_norm", "gqa_k_norm", "mla_q_norm", "mla_w_uq", "mla_kv_norm", "mla_w_ukv",
          "mlstm_conv", "mlstm_gate_b", "mlstm_head_norm", "w_out", "ln1_g", "ln1_b")


def kernel(x, c, l0_ada_w, l0_ada_b, l0_w_in, l0_gqa_q_norm, l0_gqa_k_norm, l0_mla_q_norm, l0_mla_w_uq, l0_mla_kv_norm, l0_mla_w_ukv, l0_mlstm_conv, l0_mlstm_gate_b, l0_mlstm_head_norm, l0_w_out, l0_ln1_g, l0_ln1_b, l0_ffn_w_gate, l0_ffn_w_up, l0_ffn_w_down, l0_ln2_g, l0_ln2_b, l1_ada_w, l1_ada_b, l1_w_in, l1_gqa_q_norm, l1_gqa_k_norm, l1_mla_q_norm, l1_mla_w_uq, l1_mla_kv_norm, l1_mla_w_ukv, l1_mlstm_conv, l1_mlstm_gate_b, l1_mlstm_head_norm, l1_w_out, l1_ln1_g, l1_ln1_b, l1_router_w, l1_moe_w_gate, l1_moe_w_up, l1_moe_w_down, l1_ln2_g, l1_ln2_b):
    b, s, d = x.shape
    p0 = dict(zip(_NAMES, (l0_ada_w, l0_ada_b, l0_w_in, l0_gqa_q_norm, l0_gqa_k_norm, l0_mla_q_norm, l0_mla_w_uq,
                           l0_mla_kv_norm, l0_mla_w_ukv, l0_mlstm_conv, l0_mlstm_gate_b, l0_mlstm_head_norm, l0_w_out,
                           l0_ln1_g, l0_ln1_b)))
    p0.update(ffn_w_gate=l0_ffn_w_gate, ffn_w_up=l0_ffn_w_up, ffn_w_down=l0_ffn_w_down, ln2_g=l0_ln2_g, ln2_b=l0_ln2_b)
    p1 = dict(zip(_NAMES, (l1_ada_w, l1_ada_b, l1_w_in, l1_gqa_q_norm, l1_gqa_k_norm, l1_mla_q_norm, l1_mla_w_uq,
                           l1_mla_kv_norm, l1_mla_w_ukv, l1_mlstm_conv, l1_mlstm_gate_b, l1_mlstm_head_norm, l1_w_out,
                           l1_ln1_g, l1_ln1_b)))
    p1.update(router_w=l1_router_w, moe_w_gate=l1_moe_w_gate, moe_w_up=l1_moe_w_up, moe_w_down=l1_moe_w_down,
              ln2_g=l1_ln2_g, ln2_b=l1_ln2_b)
    tabs = _rope_tables(s)
    outs = []
    for bi in range(b):
        xb = x[bi]
        c_col = c[bi].reshape(d, 1)
        xb = _layer(xb, c_col, tabs, p0, moe=False)
        xb = _layer(xb, c_col, tabs, p1, moe=True)
        outs.append(xb)
    return jnp.stack(outs, axis=0)
```

```python
import functools

import jax
import jax.numpy as jnp
from jax import lax
from jax.experimental import pallas as pl
from jax.experimental.pallas import tpu as pltpu

F32 = jnp.float32
BF16 = jnp.bfloat16

GRID_W = 64
ROPE_THETA = 10000.0
HEAD64 = 64
GQA_HEADS = 16
GQA_KV = 4
MLA_HEADS = 4
MLA_NOPE = 128
MLA_V = 128
MLA_QR = 512
MLA_KVR = 256
ML_HEADS = 4
ML_D = 128
ML_W = ML_HEADS * ML_D
CHUNK = 128
N_EXPERTS = 8
RMS_EPS = 1e-6
LN_EPS = 1e-5
DEPTH = 2
ALPHA = (2 * DEPTH) ** 0.25

LANES = 128
NEG_BIG = -1e30
LOG2E = 1.4426950408889634
ONES_ROWS = 16
VMEM_LIMIT = 56 * 1024 * 1024
FFN_TM = 1024


def _cp(*sem, vmem=VMEM_LIMIT):
    return pltpu.CompilerParams(dimension_semantics=sem, vmem_limit_bytes=vmem)


def _full(shape):
    n = len(shape)
    return pl.BlockSpec(shape, lambda *_: (0,) * n)


def _bf(x):
    return x.astype(BF16)


def _dot(a, b):
    return jnp.dot(a, b, preferred_element_type=F32)


def _split3(x):
    hi = _bf(x)
    r1 = x - hi.astype(F32)
    mid = _bf(r1)
    lo = _bf(r1 - mid.astype(F32))
    return hi, mid, lo


def _ada_kernel(c_ref, w_ref, b_ref, o_ref):
    c = c_ref[...]
    ca = c * jax.nn.sigmoid(c)
    o_ref[...] = jnp.sum(ca * w_ref[...], axis=0, keepdims=True) + b_ref[...]


def _ada(c_col, w, b):
    d, n = w.shape
    tn = 1024
    return pl.pallas_call(
        _ada_kernel,
        out_shape=jax.ShapeDtypeStruct((1, n), F32),
        grid=(n // tn,),
        in_specs=[_full((d, 1)), pl.BlockSpec((d, tn), lambda j: (0, j)), pl.BlockSpec((1, tn), lambda j: (0, j))],
        out_specs=pl.BlockSpec((1, tn), lambda j: (0, j)),
        compiler_params=_cp("parallel"),
        name="ada_mod",
    )(c_col, w, b.reshape(1, n))


def _group_sumsq(blk, bd):
    hi, mid, lo = _split3(blk * blk)
    return _dot(hi, bd) + _dot(mid, bd) + _dot(lo, bd)


def _rope(blk, cos, sin_a, sin_b):
    return blk * cos + pltpu.roll(blk, LANES - 16, 1) * sin_a + pltpu.roll(blk, 16, 1) * sin_b


def _modulate(x, scale, shift):
    return x * (1.0 + scale) + shift


def _ones_rows(n):
    r = lax.broadcasted_iota(jnp.int32, (ONES_ROWS, n), 0)
    return jnp.where(r == 0, 1.0, 0.0).astype(BF16)


def _inproj_gqa_kernel(x_ref, sc_ref, sh_ref, w_ref, qg_ref, kg_ref, cos_ref, sa_ref, sb_ref, bd_ref,
                       qT_ref, k_ref, vT_ref):
    u = _bf(_modulate(x_ref[...], sc_ref[...], sh_ref[...]))
    y = _dot(u, w_ref[...])
    cos, sa, sb, bd = cos_ref[...], sa_ref[...], sb_ref[...], bd_ref[...]
    nq = GQA_HEADS * HEAD64 // LANES
    nk = GQA_KV * HEAD64 // LANES
    scale = HEAD64 ** -0.5 * LOG2E
    for j in range(nq):
        blk = y[:, j * LANES:(j + 1) * LANES]
        ss = _group_sumsq(blk, bd)
        blk = blk * lax.rsqrt(ss * (1.0 / HEAD64) + RMS_EPS) * qg_ref[...]
        blk = _rope(blk, cos, sa, sb) * scale
        qT_ref[j * LANES:(j + 1) * LANES, :] = _bf(blk.T)
    for j in range(nk):
        blk = y[:, (nq + j) * LANES:(nq + j + 1) * LANES]
        ss = _group_sumsq(blk, bd)
        blk = blk * lax.rsqrt(ss * (1.0 / HEAD64) + RMS_EPS) * kg_ref[...]
        blk = _bf(_rope(blk, cos, sa, sb))
        k_ref[2 * j] = blk[:, :HEAD64]
        k_ref[2 * j + 1] = blk[:, HEAD64:]
    vT = _bf(y[:, (nq + nk) * LANES:].T)
    ones = _ones_rows(vT.shape[1])
    dve = HEAD64 + ONES_ROWS
    for h in range(GQA_KV):
        vT_ref[h * dve:h * dve + HEAD64, :] = vT[h * HEAD64:(h + 1) * HEAD64, :]
        vT_ref[h * dve + HEAD64:(h + 1) * dve, :] = ones


def _inproj_gqa(x, sc, sh, w, qg, kg, cos, sa, sb, bd, tm=256):
    s, d = x.shape
    n = w.shape[1]
    row = lambda i: (i, 0)
    return pl.pallas_call(
        _inproj_gqa_kernel,
        out_shape=(jax.ShapeDtypeStruct((GQA_HEADS * HEAD64, s), BF16),
                   jax.ShapeDtypeStruct((GQA_KV, s, HEAD64), BF16),
                   jax.ShapeDtypeStruct((GQA_KV * (HEAD64 + ONES_ROWS), s), BF16)),
        grid=(s // tm,),
        in_specs=[pl.BlockSpec((tm, d), row), _full((1, d)), _full((1, d)), _full((d, n)),
                  _full((1, LANES)), _full((1, LANES)),
                  pl.BlockSpec((tm, LANES), row), pl.BlockSpec((tm, LANES), row), pl.BlockSpec((tm, LANES), row),
                  _full((LANES, LANES))],
        out_specs=(pl.BlockSpec((GQA_HEADS * HEAD64, tm), lambda i: (0, i)),
                   pl.BlockSpec((GQA_KV, tm, HEAD64), lambda i: (0, i, 0)),
                   pl.BlockSpec((GQA_KV * (HEAD64 + ONES_ROWS), tm), lambda i: (0, i))),
        compiler_params=_cp("parallel"),
        name="inproj_gqa",
    )(x, sc, sh, w, qg, kg, cos, sa, sb, bd)


def _inproj_mla_kernel(x_ref, sc_ref, sh_ref, w_ref, qn_ref, kvn_ref, wuq_ref, wukv_ref, cos_ref, sa_ref, sb_ref,
                       qT_ref, k_ref, vT_ref):
    u = _bf(_modulate(x_ref[...], sc_ref[...], sh_ref[...]))
    y = _dot(u, w_ref[...])
    cos, sa, sb = cos_ref[...], sa_ref[...], sb_ref[...]
    cq = y[:, :MLA_QR]
    ckv = y[:, MLA_QR:MLA_QR + MLA_KVR]
    kr = y[:, MLA_QR + MLA_KVR:]
    cqn = cq * lax.rsqrt(jnp.mean(cq * cq, axis=-1, keepdims=True) + RMS_EPS) * qn_ref[...]
    ckvn = ckv * lax.rsqrt(jnp.mean(ckv * ckv, axis=-1, keepdims=True) + RMS_EPS) * kvn_ref[...]
    qm = _dot(_bf(cqn), wuq_ref[...])
    kvm = _dot(_bf(ckvn), wukv_ref[...])
    scale = (MLA_NOPE + HEAD64) ** -0.5 * LOG2E
    dq = MLA_NOPE + HEAD64
    dve = MLA_V + ONES_ROWS
    ones = _ones_rows(y.shape[0])
    k_rope = _bf(_rope(kr, cos, sa, sb)[:, :HEAD64])
    for p in range(MLA_HEADS // 2):
        rblk = qm[:, MLA_HEADS * MLA_NOPE + p * LANES:MLA_HEADS * MLA_NOPE + (p + 1) * LANES]
        rT = _bf((_rope(rblk, cos, sa, sb) * scale).T)
        for e in range(2):
            h = 2 * p + e
            qT_ref[h * dq + MLA_NOPE:(h + 1) * dq, :] = rT[e * HEAD64:(e + 1) * HEAD64, :]
    for h in range(MLA_HEADS):
        nope = qm[:, h * MLA_NOPE:(h + 1) * MLA_NOPE] * scale
        qT_ref[h * dq:h * dq + MLA_NOPE, :] = _bf(nope.T)
        k_ref[h, :, :MLA_NOPE] = _bf(kvm[:, h * 2 * LANES:h * 2 * LANES + MLA_NOPE])
        k_ref[h, :, MLA_NOPE:] = k_rope
        vT_ref[h * dve:h * dve + MLA_V, :] = _bf(kvm[:, h * 2 * LANES + MLA_NOPE:(h + 1) * 2 * LANES].T)
        vT_ref[h * dve + MLA_V:(h + 1) * dve, :] = ones


def _inproj_mla(x, sc, sh, w, qn, kvn, wuq, wukv, cos, sa, sb, tm=256):
    s, d = x.shape
    n = w.shape[1]
    dq = MLA_NOPE + HEAD64
    row = lambda i: (i, 0)
    return pl.pallas_call(
        _inproj_mla_kernel,
        out_shape=(jax.ShapeDtypeStruct((MLA_HEADS * dq, s), BF16),
                   jax.ShapeDtypeStruct((MLA_HEADS, s, dq), BF16),
                   jax.ShapeDtypeStruct((MLA_HEADS * (MLA_V + ONES_ROWS), s), BF16)),
        grid=(s // tm,),
        in_specs=[pl.BlockSpec((tm, d), row), _full((1, d)), _full((1, d)), _full((d, n)),
                  _full((1, MLA_QR)), _full((1, MLA_KVR)), _full(wuq.shape), _full(wukv.shape),
                  pl.BlockSpec((tm, LANES), row), pl.BlockSpec((tm, LANES), row), pl.BlockSpec((tm, LANES), row)],
        out_specs=(pl.BlockSpec((MLA_HEADS * dq, tm), lambda i: (0, i)),
                   pl.BlockSpec((MLA_HEADS, tm, dq), lambda i: (0, i, 0)),
                   pl.BlockSpec((MLA_HEADS * (MLA_V + ONES_ROWS), tm), lambda i: (0, i))),
        compiler_params=_cp("parallel"),
        name="inproj_mla",
    )(x, sc, sh, w, qn, kvn, wuq, wukv, cos, sa, sb)


def _inproj_mlstm_kernel(x_ref, xp_ref, xn_ref, sc_ref, sh_ref, w_ref, conv_ref, gb_ref,
                         q_ref, kT_ref, v_ref, o_ref, gT_ref):
    i = pl.program_id(0)
    last = pl.num_programs(0) - 1
    sc, sh = sc_ref[...], sh_ref[...]
    u = _bf(_modulate(x_ref[...], sc, sh))
    y = _dot(u, w_ref[...])
    tm = y.shape[0]
    qk = y[:, :2 * ML_W]
    halo = jnp.concatenate([xp_ref[...], xn_ref[...]], axis=0)
    hy = _dot(_bf(_modulate(halo, sc, sh)), w_ref[:, :2 * ML_W])
    prev_row = jnp.where(i == 0, 0.0, hy[7:8, :])
    next_row = jnp.where(i == last, 0.0, hy[8:9, :])
    rows = lax.broadcasted_iota(jnp.int32, qk.shape, 0)
    prev = jnp.where(rows == 0, prev_row, pltpu.roll(qk, 1, 0))
    nxt = jnp.where(rows == tm - 1, next_row, pltpu.roll(qk, tm - 1, 0))
    cw = conv_ref[...]
    z = cw[0:1, :] * prev + cw[1:2, :] * qk + cw[2:3, :] * nxt
    z = z * jax.nn.sigmoid(z)
    q_ref[...] = _bf(z[:, :ML_W])
    kT_ref[...] = (z[:, ML_W:] * (ML_D ** -0.5)).T
    v_ref[...] = _bf(y[:, 2 * ML_W:3 * ML_W])
    o_ref[...] = y[:, 3 * ML_W:4 * ML_W]
    g = y[:, 4 * ML_W:] + gb_ref[...]
    gT_ref[...] = g.T[:4 * ML_HEADS, :]


def _inproj_mlstm(x, sc, sh, w, conv, gb, tm=256):
    s, d = x.shape
    n = w.shape[1]
    row = lambda i: (i, 0)
    col = lambda i: (0, i)
    r8 = tm // 8
    return pl.pallas_call(
        _inproj_mlstm_kernel,
        out_shape=(jax.ShapeDtypeStruct((s, ML_W), BF16), jax.ShapeDtypeStruct((ML_W, s), F32),
                   jax.ShapeDtypeStruct((s, ML_W), BF16), jax.ShapeDtypeStruct((s, ML_W), F32),
                   jax.ShapeDtypeStruct((4 * ML_HEADS, s), F32)),
        grid=(s // tm,),
        in_specs=[pl.BlockSpec((tm, d), row),
                  pl.BlockSpec((8, d), lambda i: (jnp.maximum(i * r8 - 1, 0), 0)),
                  pl.BlockSpec((8, d), lambda i: (jnp.minimum((i + 1) * r8, s // 8 - 1), 0)),
                  _full((1, d)), _full((1, d)), _full((d, n)), _full((3, 2 * ML_W)), _full((1, LANES))],
        out_specs=(pl.BlockSpec((tm, ML_W), row), pl.BlockSpec((ML_W, tm), col),
                   pl.BlockSpec((tm, ML_W), row), pl.BlockSpec((tm, ML_W), row),
                   pl.BlockSpec((4 * ML_HEADS, tm), col)),
        compiler_params=_cp("parallel"),
        name="inproj_mlstm",
    )(x, x, x, sc, sh, w, conv, gb)


def _attn_kernel(qT_ref, k_ref, vT_ref, o_ref, s0_sc, s1_sc, acc_sc, *, groups, dq, dv, tk, unroll):
    seq = k_ref.shape[1]
    n = seq // tk
    if groups == 1:
        qT = qT_ref[...]
    else:
        qT = jnp.concatenate([qT_ref[g * dq:(g + 1) * dq, :] for g in range(groups)], axis=1)
    acc_sc[...] = jnp.zeros(acc_sc.shape, F32)

    def scores(c, s_sc):
        off = pl.multiple_of(c * tk, tk)
        s = _dot(k_ref[0, pl.ds(off, tk), :], qT)
        s_sc[...] = s
        return jnp.max(s, axis=0, keepdims=True)

    def accumulate(c, s_sc, mx, m_old):
        off = pl.multiple_of(c * tk, tk)
        m_new = jnp.maximum(m_old, mx)
        p = jnp.exp2(s_sc[...] - m_new)
        acc_sc[...] = jnp.exp2(m_old - m_new) * acc_sc[...] + _dot(vT_ref[:, pl.ds(off, tk)], _bf(p))
        return m_new

    def body(i, carry):
        mx, m = carry
        bufs = (s0_sc, s1_sc)
        for j in range(unroll):
            c = unroll * i + j
            mx_next = scores(jnp.minimum(c + 1, n - 1), bufs[(j + 1) % 2])
            m = accumulate(c, bufs[j % 2], mx, m)
            mx = mx_next
        return mx, m

    nq = acc_sc.shape[1]
    lax.fori_loop(0, n // unroll, body, (scores(0, s0_sc), jnp.full((1, nq), NEG_BIG, F32)))
    oT = acc_sc[:dv, :] * (1.0 / acc_sc[dv:dv + 1, :])
    c = qT_ref.shape[1]
    if groups > 1:
        oT = jnp.concatenate([oT[:, g * c:(g + 1) * c] for g in range(groups)], axis=0)
    o_ref[...] = _bf(oT.T)


def _attention(qT, k, vT, *, groups, dq, dv, cq, tk=512, unroll=4):
    h, s, _ = k.shape
    nq = groups * cq
    dve = dv + ONES_ROWS
    unroll = min(unroll, s // tk)
    assert unroll % 2 == 0 and (s // tk) % unroll == 0
    kern = functools.partial(_attn_kernel, groups=groups, dq=dq, dv=dv, tk=tk, unroll=unroll)
    return pl.pallas_call(
        kern,
        out_shape=jax.ShapeDtypeStruct((s, h * groups * dv), BF16),
        grid=(h, s // cq),
        in_specs=[pl.BlockSpec((groups * dq, cq), lambda a, i: (a, i)),
                  pl.BlockSpec((1, s, dq), lambda a, i: (a, 0, 0)),
                  pl.BlockSpec((dve, s), lambda a, i: (a, 0))],
        out_specs=pl.BlockSpec((cq, groups * dv), lambda a, i: (i, a)),
        scratch_shapes=[pltpu.VMEM((tk, nq), F32), pltpu.VMEM((tk, nq), F32), pltpu.VMEM((dve, nq), F32)],
        compiler_params=_cp("parallel", "parallel"),
        name="attention_g%d" % groups,
    )(qT, k, vT)


def _mlstm_kernel(q_ref, kT_ref, v_ref, o_ref, g_ref, hn_ref, out_ref,
                  cn_sc, mprev_sc, w_sc, r_sc, lf_sc, g_sc, mloc_sc, hacc_sc):
    head = pl.program_id(0)
    nc = g_ref.shape[1]
    L = CHUNK
    ii = lax.broadcasted_iota(jnp.int32, (L, L), 0)
    jj = lax.broadcasted_iota(jnp.int32, (L, L), 1)
    e0 = jnp.where(lax.broadcasted_iota(jnp.int32, (L, LANES), 1) == 0, 1.0, 0.0).astype(BF16)

    for direction in range(2):
        fw = direction == 0
        mask = (jj <= ii) if fw else (jj >= ii)
        tri = jnp.where((ii <= jj) if fw else (ii >= jj), 1.0, 0.0).astype(BF16)
        li = g_ref[2 * ML_HEADS * direction + head]
        fg = g_ref[2 * ML_HEADS * direction + ML_HEADS + head]
        lf = jnp.minimum(fg, 0.0) - jnp.log1p(jnp.exp(-jnp.abs(fg)))
        hi, mid, lo = _split3(lf)
        b = _dot(hi, tri) + _dot(mid, tri) + _dot(lo, tri)
        gtot = jnp.sum(lf, axis=-1, keepdims=True)
        a = gtot - b + li
        mloc = jnp.max(a, axis=-1, keepdims=True)
        w_sc[...] = jnp.exp(a - mloc)
        r_sc[...] = li - b
        lf_sc[...] = lf
        g_sc[...] = jnp.broadcast_to(gtot, (nc, LANES))
        mloc_sc[...] = jnp.broadcast_to(mloc, (nc, LANES))

        def scan_body(i, carry, fw=fw):
            cn, m = carry
            cc = i if fw else nc - 1 - i
            off = pl.multiple_of(cc * L, L)
            cn_sc[cc] = _bf(cn)
            mprev_sc[pl.ds(cc, 1), :] = jnp.broadcast_to(m, (1, LANES))
            kw = _bf(kT_ref[:, pl.ds(off, L)] * w_sc[pl.ds(cc, 1), :])
            vext = jnp.concatenate([v_ref[pl.ds(off, L), :], e0], axis=1)
            kv = _dot(kw, vext)
            g = g_sc[pl.ds(cc, 1), 0:1]
            ml = mloc_sc[pl.ds(cc, 1), 0:1]
            m_new = jnp.maximum(g + m, ml)
            cn = jnp.exp(g + m - m_new) * cn + jnp.exp(ml - m_new) * kv
            return cn, m_new

        lax.fori_loop(0, nc, scan_body, (jnp.zeros((ML_D, 2 * ML_D), F32), jnp.zeros((1, 1), F32)), unroll=2)

        def out_body(cc, carry, fw=fw, mask=mask):
            off = pl.multiple_of(cc * L, L)
            mp = mprev_sc[pl.ds(cc, 1), 0:1]
            rm = jnp.where(mask, r_sc[pl.ds(cc, 1), :], -jnp.inf)
            mt = jnp.maximum(mp, jnp.max(rm, axis=-1, keepdims=True))
            d = jnp.exp(rm - mt)
            iw = jnp.exp(mp - mt)
            qc = q_ref[pl.ds(off, L), :]
            qk = _dot(qc, _bf(kT_ref[:, pl.ds(off, L)])) * d
            num_i = _dot(_bf(qk), v_ref[pl.ds(off, L), :])
            den_i = jnp.sum(qk, axis=-1, keepdims=True)
            inter = _dot(qc, cn_sc[cc])
            num = num_i + iw * inter[:, :ML_D]
            den = den_i + iw * inter[:, ML_D:ML_D + 1]
            bcol = jnp.sum(jnp.where(mask, lf_sc[pl.ds(cc, 1), :], 0.0), axis=-1, keepdims=True)
            h = num / jnp.maximum(jnp.abs(den), jnp.exp(-(bcol + mt)))
            if fw:
                hacc_sc[pl.ds(off, L), :] = h
            else:
                h = h + hacc_sc[pl.ds(off, L), :]
                mu = jnp.mean(h, axis=-1, keepdims=True)
                hc = h - mu
                var = jnp.mean(hc * hc, axis=-1, keepdims=True)
                hn = hc * lax.rsqrt(var + LN_EPS) * hn_ref[...]
                out_ref[pl.ds(off, L), :] = _bf(jax.nn.sigmoid(o_ref[pl.ds(off, L), :]) * hn)
            return carry

        lax.fori_loop(0, nc, out_body, 0, unroll=4)


def _mlstm(q, kT, v, o, g3, hn):
    s = q.shape[0]
    nc = s // CHUNK
    colblk = lambda h: (0, h)
    return pl.pallas_call(
        _mlstm_kernel,
        out_shape=jax.ShapeDtypeStruct((s, ML_W), BF16),
        grid=(ML_HEADS,),
        in_specs=[pl.BlockSpec((s, ML_D), colblk), pl.BlockSpec((ML_D, s), lambda h: (h, 0)),
                  pl.BlockSpec((s, ML_D), colblk), pl.BlockSpec((s, ML_D), colblk),
                  _full(g3.shape), pl.BlockSpec((1, ML_D), colblk)],
        out_specs=pl.BlockSpec((s, ML_D), colblk),
        scratch_shapes=[pltpu.VMEM((nc, ML_D, 2 * ML_D), BF16), pltpu.VMEM((nc, LANES), F32),
                        pltpu.VMEM((nc, CHUNK), F32), pltpu.VMEM((nc, CHUNK), F32), pltpu.VMEM((nc, CHUNK), F32),
                        pltpu.VMEM((nc, LANES), F32), pltpu.VMEM((nc, LANES), F32),
                        pltpu.VMEM((s, ML_D), F32)],
        compiler_params=_cp("parallel"),
        name="mlstm",
    )(q, kT, v, o, g3, hn)


def _layer_norm(z, g, b):
    mu = jnp.mean(z, axis=-1, keepdims=True)
    zc = z - mu
    var = jnp.mean(zc * zc, axis=-1, keepdims=True)
    return zc * lax.rsqrt(var + LN_EPS) * g + b


def _outproj_kernel(*refs, route):
    if route:
        (x_ref, a_ref, b_ref, c_ref, w_ref, gate_ref, g_ref, bb_ref, sc_ref, sh_ref, rwh_ref, rwl_ref,
         x1_ref, u2_ref, lg_ref) = refs
    else:
        x_ref, a_ref, b_ref, c_ref, w_ref, gate_ref, g_ref, bb_ref, sc_ref, sh_ref, x1_ref, u2_ref = refs
    na, nb = a_ref.shape[1], b_ref.shape[1]
    y = _dot(a_ref[...], w_ref[:na, :]) + _dot(b_ref[...], w_ref[na:na + nb, :]) + _dot(c_ref[...], w_ref[na + nb:, :])
    x1 = _layer_norm(ALPHA * x_ref[...] + gate_ref[...] * y, g_ref[...], bb_ref[...])
    x1_ref[...] = x1
    u2 = _modulate(x1, sc_ref[...], sh_ref[...])
    u2_ref[...] = _bf(u2)
    if route:
        hi = _bf(u2)
        lo = _bf(u2 - hi.astype(F32))
        lg_ref[...] = _dot(hi, rwh_ref[...]) + (_dot(lo, rwh_ref[...]) + _dot(hi, rwl_ref[...]))


def _outproj(x, a, b, c, w, gate, g, bb, sc, sh, rw=None, tm=512):
    s, d = x.shape
    row = lambda i: (i, 0)
    route = rw is not None
    ins = [x, a, b, c, w, gate, g, bb, sc, sh]
    in_specs = [pl.BlockSpec((tm, d), row), pl.BlockSpec((tm, a.shape[1]), row), pl.BlockSpec((tm, b.shape[1]), row),
                pl.BlockSpec((tm, c.shape[1]), row), _full(w.shape)] + [_full((1, d))] * 5
    out_shape = [jax.ShapeDtypeStruct((s, d), F32), jax.ShapeDtypeStruct((s, d), BF16)]
    out_specs = [pl.BlockSpec((tm, d), row), pl.BlockSpec((tm, d), row)]
    if route:
        rwp = jnp.zeros((d, LANES), F32).at[:, :rw.shape[1]].set(rw)
        rwh = _bf(rwp)
        rwl = _bf(rwp - rwh.astype(F32))
        ins += [rwh, rwl]
        in_specs += [_full((d, LANES)), _full((d, LANES))]
        out_shape.append(jax.ShapeDtypeStruct((s, LANES), F32))
        out_specs.append(pl.BlockSpec((tm, LANES), row))
    return pl.pallas_call(
        functools.partial(_outproj_kernel, route=route),
        out_shape=tuple(out_shape),
        grid=(s // tm,),
        in_specs=in_specs,
        out_specs=tuple(out_specs),
        compiler_params=_cp("parallel"),
        name="outproj_ln1",
    )(*ins)


def _ffn_kernel(te_ref, nu_ref, x_ref, wg_ref, wu_ref, wd_ref, y_ref):
    i, j = pl.program_id(0), pl.program_id(1)
    live = i < nu_ref[0]

    def partial_out():
        x = x_ref[...]
        gt = _dot(x, _bf(wg_ref[0]))
        up = _dot(x, _bf(wu_ref[0]))
        h = _bf(gt * jax.nn.sigmoid(gt) * up)
        return _dot(h, _bf(wd_ref[0]))

    @pl.when(live & (j == 0))
    def _():
        y_ref[...] = partial_out()

    @pl.when(live & (j > 0))
    def _():
        y_ref[...] += partial_out()

    @pl.when(jnp.logical_not(live) & (j == 0))
    def _():
        y_ref[...] = jnp.zeros(y_ref.shape, F32)


def _ffn(xs, wg, wu, wd, tile_expert, n_used, tm, tf=256):
    p, d = xs.shape
    f = wg.shape[2]
    nt, nf = p // tm, f // tf

    def clamp(i, j, nu):
        live = i < nu[0]
        return jnp.where(live, i, nu[0] - 1), jnp.where(live, j, nf - 1)

    def xmap(i, j, te, nu):
        return clamp(i, j, nu)[0], 0

    def gmap(i, j, te, nu):
        ii, jj = clamp(i, j, nu)
        return te[ii], 0, jj

    def dmap(i, j, te, nu):
        ii, jj = clamp(i, j, nu)
        return te[ii], jj, 0

    return pl.pallas_call(
        _ffn_kernel,
        out_shape=jax.ShapeDtypeStruct((p, d), F32),
        grid_spec=pltpu.PrefetchScalarGridSpec(
            num_scalar_prefetch=2,
            grid=(nt, nf),
            in_specs=[pl.BlockSpec((tm, d), xmap), pl.BlockSpec((1, d, tf), gmap), pl.BlockSpec((1, d, tf), gmap),
                      pl.BlockSpec((1, tf, d), dmap)],
            out_specs=pl.BlockSpec((tm, d), lambda i, j, te, nu: (i, 0))),
        compiler_params=_cp("parallel", "arbitrary"),
        name="swiglu_ffn",
    )(tile_expert, n_used, xs, wg, wu, wd)


def _ln2_kernel(x_ref, f_ref, gate_ref, g_ref, b_ref, o_ref):
    o_ref[...] = _layer_norm(ALPHA * x_ref[...] + gate_ref[...] * f_ref[...], g_ref[...], b_ref[...])


def _ln2(x, f, gate, g, b, tm=512):
    s, d = x.shape
    row = lambda i: (i, 0)
    return pl.pallas_call(
        _ln2_kernel,
        out_shape=jax.ShapeDtypeStruct((s, d), F32),
        grid=(s // tm,),
        in_specs=[pl.BlockSpec((tm, d), row), pl.BlockSpec((tm, d), row)] + [_full((1, d))] * 3,
        out_specs=pl.BlockSpec((tm, d), row),
        compiler_params=_cp("parallel"),
        name="residual_ln2",
    )(x, f, gate, g, b)


def _row_copy(y_hbm, idx_ref, buf, sem, tile, slot, r, tm):
    return pltpu.make_async_copy(y_hbm.at[pl.ds(idx_ref[tile * tm + r], 1)], buf.at[slot, pl.ds(r, 1)], sem.at[slot])


def _ln2_combine_kernel(d1_ref, d2_ref, x_ref, info_ref, y_hbm, gate_ref, g_ref, b_ref, o_ref, a_buf, b_buf, sem):
    i = pl.program_id(0)
    n = pl.num_programs(0)
    tm = x_ref.shape[0]

    def issue(tile, slot):
        def body(r, carry):
            _row_copy(y_hbm, d1_ref, a_buf, sem, tile, slot, r, tm).start()
            _row_copy(y_hbm, d2_ref, b_buf, sem, tile, slot, r, tm).start()
            return carry
        lax.fori_loop(0, tm, body, 0)

    @pl.when(i == 0)
    def _():
        issue(0, 0)

    @pl.when(i + 1 < n)
    def _():
        issue(i + 1, (i + 1) % 2)

    slot = i % 2
    pltpu.make_async_copy(y_hbm.at[pl.ds(0, tm)], a_buf.at[slot], sem.at[slot]).wait()
    pltpu.make_async_copy(y_hbm.at[pl.ds(0, tm)], b_buf.at[slot], sem.at[slot]).wait()
    info = info_ref[...]
    f = info[:, 2:3] * a_buf[slot] + info[:, 3:4] * b_buf[slot]
    o_ref[...] = _layer_norm(ALPHA * x_ref[...] + gate_ref[...] * f, g_ref[...], b_ref[...])


def _ln2_combine(x, info, y, d1, d2, gate, g, b, tm=256):
    s, d = x.shape
    row = lambda i, *_: (i, 0)
    one = lambda i, *_: (0, 0)
    return pl.pallas_call(
        _ln2_combine_kernel,
        out_shape=jax.ShapeDtypeStruct((s, d), F32),
        grid_spec=pltpu.PrefetchScalarGridSpec(
            num_scalar_prefetch=2,
            grid=(s // tm,),
            in_specs=[pl.BlockSpec((tm, d), row), pl.BlockSpec((tm, LANES), row), pl.BlockSpec(memory_space=pl.ANY),
                      pl.BlockSpec((1, d), one), pl.BlockSpec((1, d), one), pl.BlockSpec((1, d), one)],
            out_specs=pl.BlockSpec((tm, d), row),
            scratch_shapes=[pltpu.VMEM((2, tm, d), F32), pltpu.VMEM((2, tm, d), F32), pltpu.SemaphoreType.DMA((2,))]),
        compiler_params=_cp("arbitrary"),
        name="combine_ln2",
    )(d1, d2, x, info, y, gate, g, b)


def _router_kernel(lg_ref, info_ref, cnt_ref, carry_sc):
    i = pl.program_id(0)

    @pl.when(i == 0)
    def _():
        carry_sc[...] = jnp.zeros(carry_sc.shape, F32)

    lg = lg_ref[...]
    tm = lg.shape[0]
    lane = lax.broadcasted_iota(jnp.int32, lg.shape, 1)
    lg = jnp.where(lane < N_EXPERTS, lg, -jnp.inf)
    v1 = jnp.max(lg, axis=-1, keepdims=True)
    i1 = jnp.min(jnp.where(lg == v1, lane, LANES), axis=-1, keepdims=True)
    lg2 = jnp.where(lane == i1, -jnp.inf, lg)
    v2 = jnp.max(lg2, axis=-1, keepdims=True)
    i2 = jnp.min(jnp.where(lg2 == v2, lane, LANES), axis=-1, keepdims=True)
    e21 = jnp.exp(v2 - v1)
    w1 = 1.0 / (1.0 + e21)
    w2 = e21 * w1
    oh1 = lane == i1
    oh2 = lane == i2
    oh = jnp.where(oh1 | oh2, 1.0, 0.0)
    rr = lax.broadcasted_iota(jnp.int32, (tm, tm), 0)
    cc = lax.broadcasted_iota(jnp.int32, (tm, tm), 1)
    strict = jnp.where(cc < rr, 1.0, 0.0).astype(BF16)
    before = _dot(strict, _bf(oh)) + carry_sc[...]
    r1 = jnp.sum(jnp.where(oh1, before, 0.0), axis=-1, keepdims=True)
    r2 = jnp.sum(jnp.where(oh2, before, 0.0), axis=-1, keepdims=True)
    vals = (i1.astype(F32), i2.astype(F32), w1, w2, r1, r2)
    info = jnp.zeros(lg.shape, F32)
    for n, v in enumerate(vals):
        info = jnp.where(lane == n, v, info)
    info_ref[...] = info
    carry_sc[...] += jnp.sum(oh, axis=0, keepdims=True)
    cnt_ref[...] = carry_sc[...]


def _router(logits, tm=512):
    s = logits.shape[0]
    return pl.pallas_call(
        _router_kernel,
        out_shape=(jax.ShapeDtypeStruct((s, LANES), F32), jax.ShapeDtypeStruct((1, LANES), F32)),
        grid=(s // tm,),
        in_specs=[pl.BlockSpec((tm, LANES), lambda i: (i, 0))],
        out_specs=(pl.BlockSpec((tm, LANES), lambda i: (i, 0)), _full((1, LANES))),
        scratch_shapes=[pltpu.VMEM((1, LANES), F32)],
        compiler_params=_cp("arbitrary"),
        name="router_top2",
    )(logits)


def _rope_tables(s):
    rows = s // GRID_W
    n_freq = HEAD64 // 4
    inv_freq = ROPE_THETA ** (-jnp.arange(n_freq, dtype=F32) / n_freq)
    ang_r = jnp.arange(rows, dtype=F32)[:, None] * inv_freq
    ang_c = jnp.arange(GRID_W, dtype=F32)[:, None] * inv_freq

    def table(fn):
        tr = jnp.broadcast_to(fn(ang_r)[:, None, :], (rows, GRID_W, n_freq)).reshape(s, n_freq)
        tc = jnp.broadcast_to(fn(ang_c)[None, :, :], (rows, GRID_W, n_freq)).reshape(s, n_freq)
        return jnp.concatenate([tr, tr, tc, tc] * 2, axis=-1)

    cos, sin = table(jnp.cos), table(jnp.sin)
    first_half = (jnp.arange(LANES) % 32) < 16
    return cos, jnp.where(first_half, -sin, 0.0), jnp.where(first_half, 0.0, sin)


def _moe_ffn(u2, logits, wg, wu, wd, tm):
    s, d = u2.shape
    e = wg.shape[0]
    info, counts = _router(logits)
    e1, e2 = info[:, 0].astype(jnp.int32), info[:, 1].astype(jnp.int32)
    w1, w2 = info[:, 2], info[:, 3]
    r1, r2 = info[:, 4].astype(jnp.int32), info[:, 5].astype(jnp.int32)
    cnt = counts[0, :e].astype(jnp.int32)
    tiles = (cnt + tm - 1) // tm
    tile_end = jnp.cumsum(tiles)
    start = (tile_end - tiles) * tm
    nt = (2 * s) // tm + e
    n_used = tile_end[-1:].astype(jnp.int32)
    tile_expert = jnp.sum((jnp.arange(nt)[:, None] >= tile_end[None, :]).astype(jnp.int32), axis=1)
    tile_expert = jnp.minimum(tile_expert, e - 1)
    d1, d2 = start[e1] + r1, start[e2] + r2
    tok = jnp.arange(s, dtype=jnp.int32)
    src = jnp.zeros((nt * tm,), jnp.int32).at[d1].set(tok).at[d2].set(tok)
    xs = jnp.take(u2, src, axis=0)
    y = _ffn(xs, wg, wu, wd, tile_expert, n_used, tm=tm)
    return info, y, d1, d2


def _layer(x, c_col, tabs, p, moe):
    s, d = x.shape
    cos, sa, sb = tabs
    mod = _ada(c_col, p["ada_w"], p["ada_b"])
    sh1, sc1, g1, sh2, sc2, g2 = [mod[:, i * d:(i + 1) * d] for i in range(6)]

    w_in = p["w_in"]
    n_gqa = (GQA_HEADS + 2 * GQA_KV) * HEAD64
    n_mla = MLA_QR + MLA_KVR + HEAD64
    w_gqa = _bf(w_in[:, :n_gqa])
    w_mla = _bf(jnp.pad(w_in[:, n_gqa:n_gqa + n_mla], ((0, 0), (0, LANES - HEAD64))))
    w_ml = _bf(jnp.pad(w_in[:, n_gqa + n_mla:], ((0, 0), (0, LANES - 4 * ML_HEADS))))

    ii = jnp.arange(LANES)
    bd = _bf((ii[:, None] // HEAD64 == ii[None, :] // HEAD64).astype(F32))
    qg = jnp.tile(p["gqa_q_norm"], 2).reshape(1, LANES)
    kg = jnp.tile(p["gqa_k_norm"], 2).reshape(1, LANES)
    qT, k, vT = _inproj_gqa(x, sc1, sh1, w_gqa, qg, kg, cos, sa, sb, bd)
    out_a = _attention(qT, k, vT, groups=GQA_HEADS // GQA_KV, dq=HEAD64, dv=HEAD64, cq=128)

    dq = MLA_NOPE + HEAD64
    wuq = p["mla_w_uq"].reshape(MLA_QR, MLA_HEADS, dq)
    wuq = _bf(jnp.concatenate([wuq[:, :, :MLA_NOPE].reshape(MLA_QR, -1), wuq[:, :, MLA_NOPE:].reshape(MLA_QR, -1)], axis=1))
    qTm, km, vTm = _inproj_mla(x, sc1, sh1, w_mla, p["mla_q_norm"].reshape(1, -1), p["mla_kv_norm"].reshape(1, -1),
                               wuq, _bf(p["mla_w_ukv"]), cos, sa, sb)
    out_b = _attention(qTm, km, vTm, groups=1, dq=dq, dv=MLA_V, cq=512)

    gb = jnp.pad(p["mlstm_gate_b"], (0, LANES - 4 * ML_HEADS)).reshape(1, LANES)
    mq, mkT, mv, mo, gT = _inproj_mlstm(x, sc1, sh1, w_ml, p["mlstm_conv"], gb)
    out_c = _mlstm(mq, mkT, mv, mo, gT.reshape(4 * ML_HEADS, s // CHUNK, CHUNK), p["mlstm_head_norm"].reshape(1, -1))

    ln = [p[n].reshape(1, d) for n in ("ln1_g", "ln1_b", "ln2_g", "ln2_b")]
    if moe:
        x1, u2, logits = _outproj(x, out_a, out_b, out_c, _bf(p["w_out"]), g1, ln[0], ln[1], sc2, sh2, rw=p["router_w"])
        info, y, d1, d2 = _moe_ffn(u2, logits, p["moe_w_gate"], p["moe_w_up"], p["moe_w_down"], tm=FFN_TM)
        return _ln2_combine(x1, info, y, d1, d2, g2, ln[2], ln[3])
    else:
        x1, u2 = _outproj(x, out_a, out_b, out_c, _bf(p["w_out"]), g1, ln[0], ln[1], sc2, sh2)
        nt = s // FFN_TM
        f = _ffn(u2, p["ffn_w_gate"][None], p["ffn_w_up"][None], p["ffn_w_down"][None],
                 jnp.zeros((nt,), jnp.int32), jnp.full((1,), nt, jnp.int32), tm=FFN_TM)
    return _ln2(x1, f, g2, ln[2], ln[3])


_NAMES = ("ada_w", "ada_b", "w_in", "gqa_q_norm", "gqa_k_norm", "mla_q_norm", "mla_w_uq", "mla_kv_norm", "mla_w_ukv",
          "mlstm_conv", "mlstm_gate_b", "mlstm_head_norm", "w_out", "ln1_g", "ln1_b")


def kernel(x, c, l0_ada_w, l0_ada_b, l0_w_in, l0_gqa_q_norm, l0_gqa_k_norm, l0_mla_q_norm, l0_mla_w_uq, l0_mla_kv_norm, l0_mla_w_ukv, l0_mlstm_conv, l0_mlstm_gate_b, l0_mlstm_head_norm, l0_w_out, l0_ln1_g, l0_ln1_b, l0_ffn_w_gate, l0_ffn_w_up, l0_ffn_w_down, l0_ln2_g, l0_ln2_b, l1_ada_w, l1_ada_b, l1_w_in, l1_gqa_q_norm, l1_gqa_k_norm, l1_mla_q_norm, l1_mla_w_uq, l1_mla_kv_norm, l1_mla_w_ukv, l1_mlstm_conv, l1_mlstm_gate_b, l1_mlstm_head_norm, l1_w_out, l1_ln1_g, l1_ln1_b, l1_router_w, l1_moe_w_gate, l1_moe_w_up, l1_moe_w_down, l1_ln2_g, l1_ln2_b):
    b, s, d = x.shape
    p0 = dict(zip(_NAMES, (l0_ada_w, l0_ada_b, l0_w_in, l0_gqa_q_norm, l0_gqa_k_norm, l0_mla_q_norm, l0_mla_w_uq,
                           l0_mla_kv_norm, l0_mla_w_ukv, l0_mlstm_conv, l0_mlstm_gate_b, l0_mlstm_head_norm, l0_w_out,
                           l0_ln1_g, l0_ln1_b)))
    p0.update(ffn_w_gate=l0_ffn_w_gate, ffn_w_up=l0_ffn_w_up, ffn_w_down=l0_ffn_w_down, ln2_g=l0_ln2_g, ln2_b=l0_ln2_b)
    p1 = dict(zip(_NAMES, (l1_ada_w, l1_ada_b, l1_w_in, l1_gqa_q_norm, l1_gqa_k_norm, l1_mla_q_norm, l1_mla_w_uq,
                           l1_mla_kv_norm, l1_mla_w_ukv, l1_mlstm_conv, l1_mlstm_gate_b, l1_mlstm_head_norm, l1_w_out,
                           l1_ln1_g, l1_ln1_b)))
    p1.update(router_w=l1_router_w, moe_w_gate=l1_moe_w_gate, moe_w_up=l1_moe_w_up, moe_w_down=l1_moe_w_down,
              ln2_g=l1_ln2_g, ln2_b=l1_ln2_b)
    tabs = _rope_tables(s)
    outs = []
    for bi in range(b):
        xb = x[bi]
        c_col = c[bi].reshape(d, 1)
        xb = _layer(xb, c_col, tabs, p0, moe=False)
        xb = _layer(xb, c_col, tabs, p1, moe=True)
        outs.append(xb)
    return jnp.stack(outs, axis=0)
```

```python
import functools

import jax
import jax.numpy as jnp
from jax import lax
from jax.experimental import pallas as pl
from jax.experimental.pallas import tpu as pltpu

F32 = jnp.float32
BF16 = jnp.bfloat16

GRID_W = 64
ROPE_THETA = 10000.0
HEAD64 = 64
GQA_HEADS = 16
GQA_KV = 4
MLA_HEADS = 4
MLA_NOPE = 128
MLA_V = 128
MLA_QR = 512
MLA_KVR = 256
ML_HEADS = 4
ML_D = 128
ML_W = ML_HEADS * ML_D
CHUNK = 128
N_EXPERTS = 8
RMS_EPS = 1e-6
LN_EPS = 1e-5
DEPTH = 2
ALPHA = (2 * DEPTH) ** 0.25

LANES = 128
NEG_BIG = -1e30
LOG2E = 1.4426950408889634
ONES_ROWS = 16
ATTN_BUFS = 4
ATTN_AHEAD = 1
VMEM_LIMIT = 56 * 1024 * 1024
FFN_TM = 1024


def _cp(*sem, vmem=VMEM_LIMIT):
    return pltpu.CompilerParams(dimension_semantics=sem, vmem_limit_bytes=vmem)


def _full(shape):
    n = len(shape)
    return pl.BlockSpec(shape, lambda *_: (0,) * n)


def _bf(x):
    return x.astype(BF16)


def _dot(a, b):
    return jnp.dot(a, b, preferred_element_type=F32)


def _split3(x):
    hi = _bf(x)
    r1 = x - hi.astype(F32)
    mid = _bf(r1)
    lo = _bf(r1 - mid.astype(F32))
    return hi, mid, lo


def _ada_kernel(c_ref, w_ref, b_ref, o_ref):
    c = c_ref[...]
    ca = c * jax.nn.sigmoid(c)
    o_ref[...] = jnp.sum(ca * w_ref[...], axis=0, keepdims=True) + b_ref[...]


def _ada(c_col, w, b):
    d, n = w.shape
    tn = 1024
    return pl.pallas_call(
        _ada_kernel,
        out_shape=jax.ShapeDtypeStruct((1, n), F32),
        grid=(n // tn,),
        in_specs=[_full((d, 1)), pl.BlockSpec((d, tn), lambda j: (0, j)), pl.BlockSpec((1, tn), lambda j: (0, j))],
        out_specs=pl.BlockSpec((1, tn), lambda j: (0, j)),
        compiler_params=_cp("parallel"),
        name="ada_mod",
    )(c_col, w, b.reshape(1, n))


def _group_sumsq(blk, bd):
    hi, mid, lo = _split3(blk * blk)
    return _dot(hi, bd) + _dot(mid, bd) + _dot(lo, bd)


def _rope(blk, cos, sin_a, sin_b):
    return blk * cos + pltpu.roll(blk, LANES - 16, 1) * sin_a + pltpu.roll(blk, 16, 1) * sin_b


def _modulate(x, scale, shift):
    return x * (1.0 + scale) + shift


def _ones_rows(n):
    r = lax.broadcasted_iota(jnp.int32, (ONES_ROWS, n), 0)
    return jnp.where(r == 0, 1.0, 0.0).astype(BF16)


def _inproj_gqa_kernel(x_ref, sc_ref, sh_ref, w_ref, qg_ref, kg_ref, cos_ref, sa_ref, sb_ref, bd_ref,
                       qT_ref, k_ref, vT_ref):
    u = _bf(_modulate(x_ref[...], sc_ref[...], sh_ref[...]))
    y = _dot(u, w_ref[...])
    cos, sa, sb, bd = cos_ref[...], sa_ref[...], sb_ref[...], bd_ref[...]
    nq = GQA_HEADS * HEAD64 // LANES
    nk = GQA_KV * HEAD64 // LANES
    scale = HEAD64 ** -0.5 * LOG2E
    for j in range(nq):
        blk = y[:, j * LANES:(j + 1) * LANES]
        ss = _group_sumsq(blk, bd)
        blk = blk * lax.rsqrt(ss * (1.0 / HEAD64) + RMS_EPS) * qg_ref[...]
        blk = _rope(blk, cos, sa, sb) * scale
        qT_ref[j * LANES:(j + 1) * LANES, :] = _bf(blk.T)
    for j in range(nk):
        blk = y[:, (nq + j) * LANES:(nq + j + 1) * LANES]
        ss = _group_sumsq(blk, bd)
        blk = blk * lax.rsqrt(ss * (1.0 / HEAD64) + RMS_EPS) * kg_ref[...]
        blk = _bf(_rope(blk, cos, sa, sb))
        k_ref[2 * j] = blk[:, :HEAD64]
        k_ref[2 * j + 1] = blk[:, HEAD64:]
    vT = _bf(y[:, (nq + nk) * LANES:].T)
    ones = _ones_rows(vT.shape[1])
    dve = HEAD64 + ONES_ROWS
    for h in range(GQA_KV):
        vT_ref[h * dve:h * dve + HEAD64, :] = vT[h * HEAD64:(h + 1) * HEAD64, :]
        vT_ref[h * dve + HEAD64:(h + 1) * dve, :] = ones


def _inproj_gqa(x, sc, sh, w, qg, kg, cos, sa, sb, bd, tm=256):
    s, d = x.shape
    n = w.shape[1]
    row = lambda i: (i, 0)
    return pl.pallas_call(
        _inproj_gqa_kernel,
        out_shape=(jax.ShapeDtypeStruct((GQA_HEADS * HEAD64, s), BF16),
                   jax.ShapeDtypeStruct((GQA_KV, s, HEAD64), BF16),
                   jax.ShapeDtypeStruct((GQA_KV * (HEAD64 + ONES_ROWS), s), BF16)),
        grid=(s // tm,),
        in_specs=[pl.BlockSpec((tm, d), row), _full((1, d)), _full((1, d)), _full((d, n)),
                  _full((1, LANES)), _full((1, LANES)),
                  pl.BlockSpec((tm, LANES), row), pl.BlockSpec((tm, LANES), row), pl.BlockSpec((tm, LANES), row),
                  _full((LANES, LANES))],
        out_specs=(pl.BlockSpec((GQA_HEADS * HEAD64, tm), lambda i: (0, i)),
                   pl.BlockSpec((GQA_KV, tm, HEAD64), lambda i: (0, i, 0)),
                   pl.BlockSpec((GQA_KV * (HEAD64 + ONES_ROWS), tm), lambda i: (0, i))),
        compiler_params=_cp("parallel"),
        name="inproj_gqa",
    )(x, sc, sh, w, qg, kg, cos, sa, sb, bd)


def _inproj_mla_kernel(x_ref, sc_ref, sh_ref, w_ref, qn_ref, kvn_ref, wuq_ref, wukv_ref, cos_ref, sa_ref, sb_ref,
                       qT_ref, k_ref, vT_ref):
    u = _bf(_modulate(x_ref[...], sc_ref[...], sh_ref[...]))
    y = _dot(u, w_ref[...])
    cos, sa, sb = cos_ref[...], sa_ref[...], sb_ref[...]
    cq = y[:, :MLA_QR]
    ckv = y[:, MLA_QR:MLA_QR + MLA_KVR]
    kr = y[:, MLA_QR + MLA_KVR:]
    cqn = cq * lax.rsqrt(jnp.mean(cq * cq, axis=-1, keepdims=True) + RMS_EPS) * qn_ref[...]
    ckvn = ckv * lax.rsqrt(jnp.mean(ckv * ckv, axis=-1, keepdims=True) + RMS_EPS) * kvn_ref[...]
    qm = _dot(_bf(cqn), wuq_ref[...])
    kvm = _dot(_bf(ckvn), wukv_ref[...])
    scale = (MLA_NOPE + HEAD64) ** -0.5 * LOG2E
    dq = MLA_NOPE + HEAD64
    dve = MLA_V + ONES_ROWS
    ones = _ones_rows(y.shape[0])
    k_rope = _bf(_rope(kr, cos, sa, sb)[:, :HEAD64])
    for p in range(MLA_HEADS // 2):
        rblk = qm[:, MLA_HEADS * MLA_NOPE + p * LANES:MLA_HEADS * MLA_NOPE + (p + 1) * LANES]
        rT = _bf((_rope(rblk, cos, sa, sb) * scale).T)
        for e in range(2):
            h = 2 * p + e
            qT_ref[h * dq + MLA_NOPE:(h + 1) * dq, :] = rT[e * HEAD64:(e + 1) * HEAD64, :]
    for h in range(MLA_HEADS):
        nope = qm[:, h * MLA_NOPE:(h + 1) * MLA_NOPE] * scale
        qT_ref[h * dq:h * dq + MLA_NOPE, :] = _bf(nope.T)
        k_ref[h, :, :MLA_NOPE] = _bf(kvm[:, h * 2 * LANES:h * 2 * LANES + MLA_NOPE])
        k_ref[h, :, MLA_NOPE:] = k_rope
        vT_ref[h * dve:h * dve + MLA_V, :] = _bf(kvm[:, h * 2 * LANES + MLA_NOPE:(h + 1) * 2 * LANES].T)
        vT_ref[h * dve + MLA_V:(h + 1) * dve, :] = ones


def _inproj_mla(x, sc, sh, w, qn, kvn, wuq, wukv, cos, sa, sb, tm=256):
    s, d = x.shape
    n = w.shape[1]
    dq = MLA_NOPE + HEAD64
    row = lambda i: (i, 0)
    return pl.pallas_call(
        _inproj_mla_kernel,
        out_shape=(jax.ShapeDtypeStruct((MLA_HEADS * dq, s), BF16),
                   jax.ShapeDtypeStruct((MLA_HEADS, s, dq), BF16),
                   jax.ShapeDtypeStruct((MLA_HEADS * (MLA_V + ONES_ROWS), s), BF16)),
        grid=(s // tm,),
        in_specs=[pl.BlockSpec((tm, d), row), _full((1, d)), _full((1, d)), _full((d, n)),
                  _full((1, MLA_QR)), _full((1, MLA_KVR)), _full(wuq.shape), _full(wukv.shape),
                  pl.BlockSpec((tm, LANES), row), pl.BlockSpec((tm, LANES), row), pl.BlockSpec((tm, LANES), row)],
        out_specs=(pl.BlockSpec((MLA_HEADS * dq, tm), lambda i: (0, i)),
                   pl.BlockSpec((MLA_HEADS, tm, dq), lambda i: (0, i, 0)),
                   pl.BlockSpec((MLA_HEADS * (MLA_V + ONES_ROWS), tm), lambda i: (0, i))),
        compiler_params=_cp("parallel"),
        name="inproj_mla",
    )(x, sc, sh, w, qn, kvn, wuq, wukv, cos, sa, sb)


def _inproj_mlstm_kernel(x_ref, xp_ref, xn_ref, sc_ref, sh_ref, w_ref, conv_ref, gb_ref,
                         q_ref, kT_ref, v_ref, o_ref, gT_ref):
    i = pl.program_id(0)
    last = pl.num_programs(0) - 1
    sc, sh = sc_ref[...], sh_ref[...]
    u = _bf(_modulate(x_ref[...], sc, sh))
    y = _dot(u, w_ref[...])
    tm = y.shape[0]
    qk = y[:, :2 * ML_W]
    halo = jnp.concatenate([xp_ref[...], xn_ref[...]], axis=0)
    hy = _dot(_bf(_modulate(halo, sc, sh)), w_ref[:, :2 * ML_W])
    prev_row = jnp.where(i == 0, 0.0, hy[7:8, :])
    next_row = jnp.where(i == last, 0.0, hy[8:9, :])
    rows = lax.broadcasted_iota(jnp.int32, qk.shape, 0)
    prev = jnp.where(rows == 0, prev_row, pltpu.roll(qk, 1, 0))
    nxt = jnp.where(rows == tm - 1, next_row, pltpu.roll(qk, tm - 1, 0))
    cw = conv_ref[...]
    z = cw[0:1, :] * prev + cw[1:2, :] * qk + cw[2:3, :] * nxt
    z = z * jax.nn.sigmoid(z)
    q_ref[...] = _bf(z[:, :ML_W])
    kT_ref[...] = (z[:, ML_W:] * (ML_D ** -0.5)).T
    v_ref[...] = _bf(y[:, 2 * ML_W:3 * ML_W])
    o_ref[...] = y[:, 3 * ML_W:4 * ML_W]
    g = y[:, 4 * ML_W:] + gb_ref[...]
    gT_ref[...] = g.T[:4 * ML_HEADS, :]


def _inproj_mlstm(x, sc, sh, w, conv, gb, tm=256):
    s, d = x.shape
    n = w.shape[1]
    row = lambda i: (i, 0)
    col = lambda i: (0, i)
    r8 = tm // 8
    return pl.pallas_call(
        _inproj_mlstm_kernel,
        out_shape=(jax.ShapeDtypeStruct((s, ML_W), BF16), jax.ShapeDtypeStruct((ML_W, s), F32),
                   jax.ShapeDtypeStruct((s, ML_W), BF16), jax.ShapeDtypeStruct((s, ML_W), F32),
                   jax.ShapeDtypeStruct((4 * ML_HEADS, s), F32)),
        grid=(s // tm,),
        in_specs=[pl.BlockSpec((tm, d), row),
                  pl.BlockSpec((8, d), lambda i: (jnp.maximum(i * r8 - 1, 0), 0)),
                  pl.BlockSpec((8, d), lambda i: (jnp.minimum((i + 1) * r8, s // 8 - 1), 0)),
                  _full((1, d)), _full((1, d)), _full((d, n)), _full((3, 2 * ML_W)), _full((1, LANES))],
        out_specs=(pl.BlockSpec((tm, ML_W), row), pl.BlockSpec((ML_W, tm), col),
                   pl.BlockSpec((tm, ML_W), row), pl.BlockSpec((tm, ML_W), row),
                   pl.BlockSpec((4 * ML_HEADS, tm), col)),
        compiler_params=_cp("parallel"),
        name="inproj_mlstm",
    )(x, x, x, sc, sh, w, conv, gb)


def _attn_kernel(qT_ref, k_ref, vT_ref, o_ref, s_sc, acc_sc, *, groups, dq, dv, tk):
    hp = k_ref.shape[0]
    seq = k_ref.shape[1]
    n = seq // tk
    dve = dv + ONES_ROWS
    cq = qT_ref.shape[1]
    nq = groups * cq
    qTs = []
    for hh in range(hp):
        r0 = hh * groups * dq
        qTs.append(jnp.concatenate([qT_ref[r0 + g * dq:r0 + (g + 1) * dq, :] for g in range(groups)], axis=1)
                   if groups > 1 else qT_ref[r0:r0 + dq, :])
    acc_sc[...] = jnp.zeros(acc_sc.shape, F32)

    def scores(hh, c, buf):
        off = c * tk if isinstance(c, int) else pl.multiple_of(c * tk, tk)
        s = _dot(k_ref[hh, pl.ds(off, tk), :], qTs[hh])
        s_sc[hh, buf] = s
        return jnp.max(s, axis=0, keepdims=True)

    def accumulate(hh, c, buf, mx, m_old):
        off = c * tk if isinstance(c, int) else pl.multiple_of(c * tk, tk)
        m_new = jnp.maximum(m_old, mx)
        p = jnp.exp2(_bf(s_sc[hh, buf] - m_new))
        pv = _dot(vT_ref[hh * dve:(hh + 1) * dve, pl.ds(off, tk)], p)
        acc_sc[hh] = jnp.exp2(m_old - m_new) * acc_sc[hh] + pv
        return m_new

    def body(base, carry, tail=False):
        mxs, ms = [list(t) for t in carry[0]], list(carry[1])
        for j in range(ATTN_BUFS):
            for hh in range(hp):
                if not (tail and j + ATTN_AHEAD >= ATTN_BUFS):
                    mxs[hh].append(scores(hh, base + j + ATTN_AHEAD, (j + ATTN_AHEAD) % ATTN_BUFS))
                ms[hh] = accumulate(hh, base + j, j, mxs[hh].pop(0), ms[hh])
        return tuple(tuple(t) for t in mxs), tuple(ms)

    carry = (tuple(tuple(scores(hh, c, c) for c in range(ATTN_AHEAD)) for hh in range(hp)),
             tuple(jnp.full((1, nq), NEG_BIG, F32) for _ in range(hp)))
    carry = lax.fori_loop(0, n // ATTN_BUFS - 1, lambda i, cr: body(i * ATTN_BUFS, cr), carry)
    body(n - ATTN_BUFS, carry, tail=True)
    for hh in range(hp):
        oT = acc_sc[hh, :dv, :] * (1.0 / acc_sc[hh, dv:dv + 1, :])
        if groups > 1:
            oT = jnp.concatenate([oT[:, g * cq:(g + 1) * cq] for g in range(groups)], axis=0)
        o_ref[:, hh * groups * dv:(hh + 1) * groups * dv] = _bf(oT.T)


def _attention(qT, k, vT, *, groups, dq, dv, cq, hp, tk=512):
    h, s, _ = k.shape
    nq = groups * cq
    dve = dv + ONES_ROWS
    assert (s // tk) % ATTN_BUFS == 0 and h % hp == 0
    kern = functools.partial(_attn_kernel, groups=groups, dq=dq, dv=dv, tk=tk)
    return pl.pallas_call(
        kern,
        out_shape=jax.ShapeDtypeStruct((s, h * groups * dv), BF16),
        grid=(h // hp, s // cq),
        in_specs=[pl.BlockSpec((hp * groups * dq, cq), lambda a, i: (a, i)),
                  pl.BlockSpec((hp, s, dq), lambda a, i: (a, 0, 0)),
                  pl.BlockSpec((hp * dve, s), lambda a, i: (a, 0))],
        out_specs=pl.BlockSpec((cq, hp * groups * dv), lambda a, i: (i, a)),
        scratch_shapes=[pltpu.VMEM((hp, ATTN_BUFS, tk, nq), F32), pltpu.VMEM((hp, dve, nq), F32)],
        compiler_params=_cp("parallel", "parallel"),
        name="attention_g%d" % groups,
    )(qT, k, vT)


def _mlstm_kernel(q_ref, kT_ref, v_ref, o_ref, g_ref, hn_ref, out_ref,
                  cn_sc, mprev_sc, w_sc, r_sc, lf_sc, g_sc, mloc_sc, hacc_sc):
    head = pl.program_id(0)
    nc = g_ref.shape[1]
    L = CHUNK
    ii = lax.broadcasted_iota(jnp.int32, (L, L), 0)
    jj = lax.broadcasted_iota(jnp.int32, (L, L), 1)
    e0 = jnp.where(lax.broadcasted_iota(jnp.int32, (L, LANES), 1) == 0, 1.0, 0.0).astype(BF16)

    for direction in range(2):
        fw = direction == 0
        mask = (jj <= ii) if fw else (jj >= ii)
        tri = jnp.where((ii <= jj) if fw else (ii >= jj), 1.0, 0.0).astype(BF16)
        li = g_ref[2 * ML_HEADS * direction + head]
        fg = g_ref[2 * ML_HEADS * direction + ML_HEADS + head]
        lf = jnp.minimum(fg, 0.0) - jnp.log1p(jnp.exp(-jnp.abs(fg)))
        hi, mid, lo = _split3(lf)
        b = _dot(hi, tri) + _dot(mid, tri) + _dot(lo, tri)
        gtot = jnp.sum(lf, axis=-1, keepdims=True)
        a = gtot - b + li
        mloc = jnp.max(a, axis=-1, keepdims=True)
        w_sc[...] = jnp.exp(a - mloc)
        r_sc[...] = li - b
        lf_sc[...] = lf
        g_sc[...] = jnp.broadcast_to(gtot, (nc, LANES))
        mloc_sc[...] = jnp.broadcast_to(mloc, (nc, LANES))

        def scan_body(i, carry, fw=fw):
            cn, m = carry
            cc = i if fw else nc - 1 - i
            off = pl.multiple_of(cc * L, L)
            cn_sc[cc] = _bf(cn)
            mprev_sc[pl.ds(cc, 1), :] = jnp.broadcast_to(m, (1, LANES))
            kw = _bf(kT_ref[:, pl.ds(off, L)] * w_sc[pl.ds(cc, 1), :])
            vext = jnp.concatenate([v_ref[pl.ds(off, L), :], e0], axis=1)
            kv = _dot(kw, vext)
            g = g_sc[pl.ds(cc, 1), 0:1]
            ml = mloc_sc[pl.ds(cc, 1), 0:1]
            m_new = jnp.maximum(g + m, ml)
            cn = jnp.exp(g + m - m_new) * cn + jnp.exp(ml - m_new) * kv
            return cn, m_new

        lax.fori_loop(0, nc, scan_body, (jnp.zeros((ML_D, 2 * ML_D), F32), jnp.zeros((1, 1), F32)), unroll=2)

        def out_body(cc, carry, fw=fw, mask=mask):
            off = pl.multiple_of(cc * L, L)
            mp = mprev_sc[pl.ds(cc, 1), 0:1]
            rm = jnp.where(mask, r_sc[pl.ds(cc, 1), :], -jnp.inf)
            mt = jnp.maximum(mp, jnp.max(rm, axis=-1, keepdims=True))
            d = jnp.exp(rm - mt)
            iw = jnp.exp(mp - mt)
            qc = q_ref[pl.ds(off, L), :]
            qk = _dot(qc, _bf(kT_ref[:, pl.ds(off, L)])) * d
            num_i = _dot(_bf(qk), v_ref[pl.ds(off, L), :])
            den_i = jnp.sum(qk, axis=-1, keepdims=True)
            inter = _dot(qc, cn_sc[cc])
            num = num_i + iw * inter[:, :ML_D]
            den = den_i + iw * inter[:, ML_D:ML_D + 1]
            bcol = jnp.sum(jnp.where(mask, lf_sc[pl.ds(cc, 1), :], 0.0), axis=-1, keepdims=True)
            h = num / jnp.maximum(jnp.abs(den), jnp.exp(-(bcol + mt)))
            if fw:
                hacc_sc[pl.ds(off, L), :] = h
            else:
                h = h + hacc_sc[pl.ds(off, L), :]
                mu = jnp.mean(h, axis=-1, keepdims=True)
                hc = h - mu
                var = jnp.mean(hc * hc, axis=-1, keepdims=True)
                hn = hc * lax.rsqrt(var + LN_EPS) * hn_ref[...]
                out_ref[pl.ds(off, L), :] = _bf(jax.nn.sigmoid(o_ref[pl.ds(off, L), :]) * hn)
            return carry

        lax.fori_loop(0, nc, out_body, 0, unroll=4)


def _mlstm(q, kT, v, o, g3, hn):
    s = q.shape[0]
    nc = s // CHUNK
    colblk = lambda h: (0, h)
    return pl.pallas_call(
        _mlstm_kernel,
        out_shape=jax.ShapeDtypeStruct((s, ML_W), BF16),
        grid=(ML_HEADS,),
        in_specs=[pl.BlockSpec((s, ML_D), colblk), pl.BlockSpec((ML_D, s), lambda h: (h, 0)),
                  pl.BlockSpec((s, ML_D), colblk), pl.BlockSpec((s, ML_D), colblk),
                  _full(g3.shape), pl.BlockSpec((1, ML_D), colblk)],
        out_specs=pl.BlockSpec((s, ML_D), colblk),
        scratch_shapes=[pltpu.VMEM((nc, ML_D, 2 * ML_D), BF16), pltpu.VMEM((nc, LANES), F32),
                        pltpu.VMEM((nc, CHUNK), F32), pltpu.VMEM((nc, CHUNK), F32), pltpu.VMEM((nc, CHUNK), F32),
                        pltpu.VMEM((nc, LANES), F32), pltpu.VMEM((nc, LANES), F32),
                        pltpu.VMEM((s, ML_D), F32)],
        compiler_params=_cp("parallel"),
        name="mlstm",
    )(q, kT, v, o, g3, hn)


def _layer_norm(z, g, b):
    mu = jnp.mean(z, axis=-1, keepdims=True)
    zc = z - mu
    var = jnp.mean(zc * zc, axis=-1, keepdims=True)
    return zc * lax.rsqrt(var + LN_EPS) * g + b


def _outproj_kernel(*refs, route):
    if route:
        (x_ref, a_ref, b_ref, c_ref, w_ref, gate_ref, g_ref, bb_ref, sc_ref, sh_ref, rwh_ref, rwl_ref,
         x1_ref, u2_ref, lg_ref) = refs
    else:
        x_ref, a_ref, b_ref, c_ref, w_ref, gate_ref, g_ref, bb_ref, sc_ref, sh_ref, x1_ref, u2_ref = refs
    na, nb = a_ref.shape[1], b_ref.shape[1]
    y = _dot(a_ref[...], w_ref[:na, :]) + _dot(b_ref[...], w_ref[na:na + nb, :]) + _dot(c_ref[...], w_ref[na + nb:, :])
    x1 = _layer_norm(ALPHA * x_ref[...] + gate_ref[...] * y, g_ref[...], bb_ref[...])
    x1_ref[...] = x1
    u2 = _modulate(x1, sc_ref[...], sh_ref[...])
    u2_ref[...] = u2.astype(u2_ref.dtype)
    if route:
        hi = _bf(u2)
        lo = _bf(u2 - hi.astype(F32))
        lg_ref[...] = _dot(hi, rwh_ref[...]) + (_dot(lo, rwh_ref[...]) + _dot(hi, rwl_ref[...]))


def _outproj(x, a, b, c, w, gate, g, bb, sc, sh, rw=None, tm=512):
    s, d = x.shape
    row = lambda i: (i, 0)
    route = rw is not None
    ins = [x, a, b, c, w, gate, g, bb, sc, sh]
    in_specs = [pl.BlockSpec((tm, d), row), pl.BlockSpec((tm, a.shape[1]), row), pl.BlockSpec((tm, b.shape[1]), row),
                pl.BlockSpec((tm, c.shape[1]), row), _full(w.shape)] + [_full((1, d))] * 5
    out_shape = [jax.ShapeDtypeStruct((s, d), F32), jax.ShapeDtypeStruct((s, d), F32 if route else BF16)]
    out_specs = [pl.BlockSpec((tm, d), row), pl.BlockSpec((tm, d), row)]
    if route:
        rwp = jnp.zeros((d, LANES), F32).at[:, :rw.shape[1]].set(rw)
        rwh = _bf(rwp)
        rwl = _bf(rwp - rwh.astype(F32))
        ins += [rwh, rwl]
        in_specs += [_full((d, LANES)), _full((d, LANES))]
        out_shape.append(jax.ShapeDtypeStruct((s, LANES), F32))
        out_specs.append(pl.BlockSpec((tm, LANES), row))
    return pl.pallas_call(
        functools.partial(_outproj_kernel, route=route),
        out_shape=tuple(out_shape),
        grid=(s // tm,),
        in_specs=in_specs,
        out_specs=tuple(out_specs),
        compiler_params=_cp("parallel"),
        name="outproj_ln1",
    )(*ins)


def _swiglu_partial(x, wg, wu, wd):
    gt = _dot(x, _bf(wg))
    up = _dot(x, _bf(wu))
    return _dot(_bf(gt * jax.nn.sigmoid(gt) * up), _bf(wd))


def _ffn_dense_kernel(x_ref, wg_ref, wu_ref, wd_ref, y_ref):
    j = pl.program_id(1)

    @pl.when(j == 0)
    def _():
        y_ref[...] = _swiglu_partial(x_ref[...], wg_ref[...], wu_ref[...], wd_ref[...])

    @pl.when(j > 0)
    def _():
        y_ref[...] += _swiglu_partial(x_ref[...], wg_ref[...], wu_ref[...], wd_ref[...])


def _ffn_dense(x, wg, wu, wd, tm=FFN_TM, tf=256):
    s, d = x.shape
    f = wg.shape[1]
    return pl.pallas_call(
        _ffn_dense_kernel,
        out_shape=jax.ShapeDtypeStruct((s, d), F32),
        grid=(s // tm, f // tf),
        in_specs=[pl.BlockSpec((tm, d), lambda i, j: (i, 0)), pl.BlockSpec((d, tf), lambda i, j: (0, j)),
                  pl.BlockSpec((d, tf), lambda i, j: (0, j)), pl.BlockSpec((tf, d), lambda i, j: (j, 0))],
        out_specs=pl.BlockSpec((tm, d), lambda i, j: (i, 0)),
        compiler_params=_cp("parallel", "arbitrary"),
        name="swiglu_dense",
    )(x, wg, wu, wd)


def _ffn_routed_kernel(te_ref, nu_ref, src_ref, u_hbm, wg_ref, wu_ref, wd_ref, y_ref, xf_buf, xb_buf, sem):
    i, j = pl.program_id(0), pl.program_id(1)
    n_live = nu_ref[0]
    live = i < n_live
    tm = y_ref.shape[0]

    def row_copy(tile, r):
        return pltpu.make_async_copy(u_hbm.at[pl.ds(src_ref[tile * tm + r], 1)], xf_buf.at[pl.ds(r, 1)], sem)

    def issue(tile):
        def body(r, carry):
            row_copy(tile, r).start()
            return carry
        lax.fori_loop(0, tm, body, 0)

    @pl.when((i == 0) & (j == 0))
    def _():
        issue(0)

    @pl.when(live & (j == 0))
    def _():
        pltpu.make_async_copy(u_hbm.at[pl.ds(0, tm)], xf_buf, sem).wait()
        xb_buf[...] = _bf(xf_buf[...])
        y_ref[...] = _swiglu_partial(xb_buf[...], wg_ref[0], wu_ref[0], wd_ref[0])

    @pl.when((j == 1) & (i + 1 < n_live))
    def _():
        issue(i + 1)

    @pl.when(live & (j > 0))
    def _():
        y_ref[...] += _swiglu_partial(xb_buf[...], wg_ref[0], wu_ref[0], wd_ref[0])

    @pl.when(jnp.logical_not(live) & (j == 0))
    def _():
        y_ref[...] = jnp.zeros(y_ref.shape, F32)


def _ffn_routed(u, src, wg, wu, wd, tile_expert, n_used, tm, tf=256):
    s, d = u.shape
    f = wg.shape[2]
    nt, nf = src.shape[0] // tm, f // tf
    assert nf >= 2

    def clamp(i, j, nu):
        live = i < nu[0]
        return jnp.where(live, i, nu[0] - 1), jnp.where(live, j, nf - 1)

    def gmap(i, j, te, nu, sr):
        ii, jj = clamp(i, j, nu)
        return te[ii], 0, jj

    def dmap(i, j, te, nu, sr):
        ii, jj = clamp(i, j, nu)
        return te[ii], jj, 0

    return pl.pallas_call(
        _ffn_routed_kernel,
        out_shape=jax.ShapeDtypeStruct((nt * tm, d), F32),
        grid_spec=pltpu.PrefetchScalarGridSpec(
            num_scalar_prefetch=3,
            grid=(nt, nf),
            in_specs=[pl.BlockSpec(memory_space=pl.ANY), pl.BlockSpec((1, d, tf), gmap), pl.BlockSpec((1, d, tf), gmap),
                      pl.BlockSpec((1, tf, d), dmap)],
            out_specs=pl.BlockSpec((tm, d), lambda i, j, te, nu, sr: (i, 0)),
            scratch_shapes=[pltpu.VMEM((tm, d), F32), pltpu.VMEM((tm, d), BF16), pltpu.SemaphoreType.DMA(())]),
        compiler_params=_cp("arbitrary", "arbitrary"),
        name="swiglu_routed",
    )(tile_expert, n_used, src, u, wg, wu, wd)


def _ln2_kernel(x_ref, f_ref, gate_ref, g_ref, b_ref, o_ref):
    o_ref[...] = _layer_norm(ALPHA * x_ref[...] + gate_ref[...] * f_ref[...], g_ref[...], b_ref[...])


def _ln2(x, f, gate, g, b, tm=512):
    s, d = x.shape
    row = lambda i: (i, 0)
    return pl.pallas_call(
        _ln2_kernel,
        out_shape=jax.ShapeDtypeStruct((s, d), F32),
        grid=(s // tm,),
        in_specs=[pl.BlockSpec((tm, d), row), pl.BlockSpec((tm, d), row)] + [_full((1, d))] * 3,
        out_specs=pl.BlockSpec((tm, d), row),
        compiler_params=_cp("parallel"),
        name="residual_ln2",
    )(x, f, gate, g, b)


def _row_copy(y_hbm, idx_ref, buf, sem, tile, slot, r, tm):
    return pltpu.make_async_copy(y_hbm.at[pl.ds(idx_ref[tile * tm + r], 1)], buf.at[slot, pl.ds(r, 1)], sem.at[slot])


def _ln2_combine_kernel(d1_ref, d2_ref, x_ref, info_ref, y_hbm, gate_ref, g_ref, b_ref, o_ref, a_buf, b_buf, sem):
    i = pl.program_id(0)
    n = pl.num_programs(0)
    tm = x_ref.shape[0]

    def issue(tile, slot):
        def body(r, carry):
            _row_copy(y_hbm, d1_ref, a_buf, sem, tile, slot, r, tm).start()
            _row_copy(y_hbm, d2_ref, b_buf, sem, tile, slot, r, tm).start()
            return carry
        lax.fori_loop(0, tm, body, 0)

    @pl.when(i == 0)
    def _():
        issue(0, 0)

    @pl.when(i + 1 < n)
    def _():
        issue(i + 1, (i + 1) % 2)

    slot = i % 2
    pltpu.make_async_copy(y_hbm.at[pl.ds(0, tm)], a_buf.at[slot], sem.at[slot]).wait()
    pltpu.make_async_copy(y_hbm.at[pl.ds(0, tm)], b_buf.at[slot], sem.at[slot]).wait()
    info = info_ref[...]
    f = info[:, 2:3] * a_buf[slot] + info[:, 3:4] * b_buf[slot]
    o_ref[...] = _layer_norm(ALPHA * x_ref[...] + gate_ref[...] * f, g_ref[...], b_ref[...])


def _ln2_combine(x, info, y, d1, d2, gate, g, b, tm=256):
    s, d = x.shape
    row = lambda i, *_: (i, 0)
    one = lambda i, *_: (0, 0)
    return pl.pallas_call(
        _ln2_combine_kernel,
        out_shape=jax.ShapeDtypeStruct((s, d), F32),
        grid_spec=pltpu.PrefetchScalarGridSpec(
            num_scalar_prefetch=2,
            grid=(s // tm,),
            in_specs=[pl.BlockSpec((tm, d), row), pl.BlockSpec((tm, LANES), row), pl.BlockSpec(memory_space=pl.ANY),
                      pl.BlockSpec((1, d), one), pl.BlockSpec((1, d), one), pl.BlockSpec((1, d), one)],
            out_specs=pl.BlockSpec((tm, d), row),
            scratch_shapes=[pltpu.VMEM((2, tm, d), F32), pltpu.VMEM((2, tm, d), F32), pltpu.SemaphoreType.DMA((2,))]),
        compiler_params=_cp("arbitrary"),
        name="combine_ln2",
    )(d1, d2, x, info, y, gate, g, b)


def _router_kernel(lg_ref, info_ref, cnt_ref, carry_sc):
    i = pl.program_id(0)

    @pl.when(i == 0)
    def _():
        carry_sc[...] = jnp.zeros(carry_sc.shape, F32)

    lg = lg_ref[...]
    tm = lg.shape[0]
    lane = lax.broadcasted_iota(jnp.int32, lg.shape, 1)
    lg = jnp.where(lane < N_EXPERTS, lg, -jnp.inf)
    v1 = jnp.max(lg, axis=-1, keepdims=True)
    i1 = jnp.min(jnp.where(lg == v1, lane, LANES), axis=-1, keepdims=True)
    lg2 = jnp.where(lane == i1, -jnp.inf, lg)
    v2 = jnp.max(lg2, axis=-1, keepdims=True)
    i2 = jnp.min(jnp.where(lg2 == v2, lane, LANES), axis=-1, keepdims=True)
    e21 = jnp.exp(v2 - v1)
    w1 = 1.0 / (1.0 + e21)
    w2 = e21 * w1
    oh1 = lane == i1
    oh2 = lane == i2
    oh = jnp.where(oh1 | oh2, 1.0, 0.0)
    rr = lax.broadcasted_iota(jnp.int32, (tm, tm), 0)
    cc = lax.broadcasted_iota(jnp.int32, (tm, tm), 1)
    strict = jnp.where(cc < rr, 1.0, 0.0).astype(BF16)
    before = _dot(strict, _bf(oh)) + carry_sc[...]
    r1 = jnp.sum(jnp.where(oh1, before, 0.0), axis=-1, keepdims=True)
    r2 = jnp.sum(jnp.where(oh2, before, 0.0), axis=-1, keepdims=True)
    vals = (i1.astype(F32), i2.astype(F32), w1, w2, r1, r2)
    info = jnp.zeros(lg.shape, F32)
    for n, v in enumerate(vals):
        info = jnp.where(lane == n, v, info)
    info_ref[...] = info
    carry_sc[...] += jnp.sum(oh, axis=0, keepdims=True)
    cnt_ref[...] = carry_sc[...]


def _router(logits, tm=512):
    s = logits.shape[0]
    return pl.pallas_call(
        _router_kernel,
        out_shape=(jax.ShapeDtypeStruct((s, LANES), F32), jax.ShapeDtypeStruct((1, LANES), F32)),
        grid=(s // tm,),
        in_specs=[pl.BlockSpec((tm, LANES), lambda i: (i, 0))],
        out_specs=(pl.BlockSpec((tm, LANES), lambda i: (i, 0)), _full((1, LANES))),
        scratch_shapes=[pltpu.VMEM((1, LANES), F32)],
        compiler_params=_cp("arbitrary"),
        name="router_top2",
    )(logits)


def _rope_tables(s):
    rows = s // GRID_W
    n_freq = HEAD64 // 4
    inv_freq = ROPE_THETA ** (-jnp.arange(n_freq, dtype=F32) / n_freq)
    ang_r = jnp.arange(rows, dtype=F32)[:, None] * inv_freq
    ang_c = jnp.arange(GRID_W, dtype=F32)[:, None] * inv_freq

    def table(fn):
        tr = jnp.broadcast_to(fn(ang_r)[:, None, :], (rows, GRID_W, n_freq)).reshape(s, n_freq)
        tc = jnp.broadcast_to(fn(ang_c)[None, :, :], (rows, GRID_W, n_freq)).reshape(s, n_freq)
        return jnp.concatenate([tr, tr, tc, tc] * 2, axis=-1)

    cos, sin = table(jnp.cos), table(jnp.sin)
    first_half = (jnp.arange(LANES) % 32) < 16
    return cos, jnp.where(first_half, -sin, 0.0), jnp.where(first_half, 0.0, sin)


def _moe_ffn(u2, logits, wg, wu, wd, tm):
    s, d = u2.shape
    e = wg.shape[0]
    info, counts = _router(logits)
    e1, e2 = info[:, 0].astype(jnp.int32), info[:, 1].astype(jnp.int32)
    w1, w2 = info[:, 2], info[:, 3]
    r1, r2 = info[:, 4].astype(jnp.int32), info[:, 5].astype(jnp.int32)
    cnt = counts[0, :e].astype(jnp.int32)
    tiles = (cnt + tm - 1) // tm
    tile_end = jnp.cumsum(tiles)
    start = (tile_end - tiles) * tm
    nt = (2 * s) // tm + e
    n_used = tile_end[-1:].astype(jnp.int32)
    tile_expert = jnp.sum((jnp.arange(nt)[:, None] >= tile_end[None, :]).astype(jnp.int32), axis=1)
    tile_expert = jnp.minimum(tile_expert, e - 1)
    d1, d2 = start[e1] + r1, start[e2] + r2
    tok = jnp.arange(s, dtype=jnp.int32)
    src = jnp.zeros((nt * tm,), jnp.int32).at[d1].set(tok).at[d2].set(tok)
    y = _ffn_routed(u2, src, wg, wu, wd, tile_expert, n_used, tm=tm)
    return info, y, d1, d2


def _layer(x, c_col, tabs, p, moe):
    s, d = x.shape
    cos, sa, sb = tabs
    mod = _ada(c_col, p["ada_w"], p["ada_b"])
    sh1, sc1, g1, sh2, sc2, g2 = [mod[:, i * d:(i + 1) * d] for i in range(6)]

    w_in = p["w_in"]
    n_gqa = (GQA_HEADS + 2 * GQA_KV) * HEAD64
    n_mla = MLA_QR + MLA_KVR + HEAD64
    w_gqa = _bf(w_in[:, :n_gqa])
    w_mla = _bf(jnp.pad(w_in[:, n_gqa:n_gqa + n_mla], ((0, 0), (0, LANES - HEAD64))))
    w_ml = _bf(jnp.pad(w_in[:, n_gqa + n_mla:], ((0, 0), (0, LANES - 4 * ML_HEADS))))

    ii = jnp.arange(LANES)
    bd = _bf((ii[:, None] // HEAD64 == ii[None, :] // HEAD64).astype(F32))
    qg = jnp.tile(p["gqa_q_norm"], 2).reshape(1, LANES)
    kg = jnp.tile(p["gqa_k_norm"], 2).reshape(1, LANES)
    qT, k, vT = _inproj_gqa(x, sc1, sh1, w_gqa, qg, kg, cos, sa, sb, bd)
    out_a = _attention(qT, k, vT, groups=GQA_HEADS // GQA_KV, dq=HEAD64, dv=HEAD64, cq=128, hp=4)

    dq = MLA_NOPE + HEAD64
    wuq = p["mla_w_uq"].reshape(MLA_QR, MLA_HEADS, dq)
    wuq = _bf(jnp.concatenate([wuq[:, :, :MLA_NOPE].reshape(MLA_QR, -1), wuq[:, :, MLA_NOPE:].reshape(MLA_QR, -1)], axis=1))
    qTm, km, vTm = _inproj_mla(x, sc1, sh1, w_mla, p["mla_q_norm"].reshape(1, -1), p["mla_kv_norm"].reshape(1, -1),
                               wuq, _bf(p["mla_w_ukv"]), cos, sa, sb)
    out_b = _attention(qTm, km, vTm, groups=1, dq=dq, dv=MLA_V, cq=512, hp=2)

    gb = jnp.pad(p["mlstm_gate_b"], (0, LANES - 4 * ML_HEADS)).reshape(1, LANES)
    mq, mkT, mv, mo, gT = _inproj_mlstm(x, sc1, sh1, w_ml, p["mlstm_conv"], gb)
    out_c = _mlstm(mq, mkT, mv, mo, gT.reshape(4 * ML_HEADS, s // CHUNK, CHUNK), p["mlstm_head_norm"].reshape(1, -1))

    ln = [p[n].reshape(1, d) for n in ("ln1_g", "ln1_b", "ln2_g", "ln2_b")]
    if moe:
        x1, u2, logits = _outproj(x, out_a, out_b, out_c, _bf(p["w_out"]), g1, ln[0], ln[1], sc2, sh2, rw=p["router_w"])
        info, y, d1, d2 = _moe_ffn(u2, logits, p["moe_w_gate"], p["moe_w_up"], p["moe_w_down"], tm=FFN_TM)
        return _ln2_combine(x1, info, y, d1, d2, g2, ln[2], ln[3])
    else:
        x1, u2 = _outproj(x, out_a, out_b, out_c, _bf(p["w_out"]), g1, ln[0], ln[1], sc2, sh2)
        f = _ffn_dense(u2, p["ffn_w_gate"], p["ffn_w_up"], p["ffn_w_down"])
    return _ln2(x1, f, g2, ln[2], ln[3])


_NAMES = ("ada_w", "ada_b", "w_in", "gqa_q_norm", "gqa_k_norm", "mla_q_norm", "mla_w_uq", "mla_kv_norm", "mla_w_ukv",
          "mlstm_conv", "mlstm_gate_b", "mlstm_head_norm", "w_out", "ln1_g", "ln1_b")


def kernel(x, c, l0_ada_w, l0_ada_b, l0_w_in, l0_gqa_q_norm, l0_gqa_k_norm, l0_mla_q_norm, l0_mla_w_uq, l0_mla_kv_norm, l0_mla_w_ukv, l0_mlstm_conv, l0_mlstm_gate_b, l0_mlstm_head_norm, l0_w_out, l0_ln1_g, l0_ln1_b, l0_ffn_w_gate, l0_ffn_w_up, l0_ffn_w_down, l0_ln2_g, l0_ln2_b, l1_ada_w, l1_ada_b, l1_w_in, l1_gqa_q_norm, l1_gqa_k_norm, l1_mla_q_norm, l1_mla_w_uq, l1_mla_kv_norm, l1_mla_w_ukv, l1_mlstm_conv, l1_mlstm_gate_b, l1_mlstm_head_norm, l1_w_out, l1_ln1_g, l1_ln1_b, l1_router_w, l1_moe_w_gate, l1_moe_w_up, l1_moe_w_down, l1_ln2_g, l1_ln2_b):
    b, s, d = x.shape
    p0 = dict(zip(_NAMES, (l0_ada_w, l0_ada_b, l0_w_in, l0_gqa_q_norm, l0_gqa_k_norm, l0_mla_q_norm, l0_mla_w_uq,
                           l0_mla_kv_norm, l0_mla_w_ukv, l0_mlstm_conv, l0_mlstm_gate_b, l0_mlstm_head_norm, l0_w_out,
                           l0_ln1_g, l0_ln1_b)))
    p0.update(ffn_w_gate=l0_ffn_w_gate, ffn_w_up=l0_ffn_w_up, ffn_w_down=l0_ffn_w_down, ln2_g=l0_ln2_g, ln2_b=l0_ln2_b)
    p1 = dict(zip(_NAMES, (l1_ada_w, l1_ada_b, l1_w_in, l1_gqa_q_norm, l1_gqa_k_norm, l1_mla_q_norm, l1_mla_w_uq,
                           l1_mla_kv_norm, l1_mla_w_ukv, l1_mlstm_conv, l1_mlstm_gate_b, l1_mlstm_head_norm, l1_w_out,
                           l1_ln1_g, l1_ln1_b)))
    p1.update(router_w=l1_router_w, moe_w_gate=l1_moe_w_gate, moe_w_up=l1_moe_w_up, moe_w_down=l1_moe_w_down,
              ln2_g=l1_ln2_g, ln2_b=l1_ln2_b)
    tabs = _rope_tables(s)
    outs = []
    for bi in range(b):
        xb = x[bi]
        c_col = c[bi].reshape(d, 1)
        xb = _layer(xb, c_col, tabs, p0, moe=False)
        xb = _layer(xb, c_col, tabs, p1, moe=True)
        outs.append(xb)
    return jnp.stack(outs, axis=0)
```

```python
import functools

import jax
import jax.numpy as jnp
from jax import lax
from jax.experimental import pallas as pl
from jax.experimental.pallas import tpu as pltpu

F32 = jnp.float32
BF16 = jnp.bfloat16

GRID_W = 64
ROPE_THETA = 10000.0
HEAD64 = 64
GQA_HEADS = 16
GQA_KV = 4
MLA_HEADS = 4
MLA_NOPE = 128
MLA_V = 128
MLA_QR = 512
MLA_KVR = 256
ML_HEADS = 4
ML_D = 128
ML_W = ML_HEADS * ML_D
CHUNK = 128
N_EXPERTS = 8
RMS_EPS = 1e-6
LN_EPS = 1e-5
DEPTH = 2
ALPHA = (2 * DEPTH) ** 0.25

LANES = 128
NEG_BIG = -1e30
LOG2E = 1.4426950408889634
ONES_ROWS = 16
ATTN_BUFS = 4
ATTN_AHEAD = 1
VMEM_LIMIT = 56 * 1024 * 1024
FFN_TM = 1024
GATHER_ROWS_PER_STEP = 64


def _cp(*sem, vmem=VMEM_LIMIT):
    return pltpu.CompilerParams(dimension_semantics=sem, vmem_limit_bytes=vmem)


def _full(shape):
    n = len(shape)
    return pl.BlockSpec(shape, lambda *_: (0,) * n)


def _bf(x):
    return x.astype(BF16)


def _dot(a, b):
    return jnp.dot(a, b, preferred_element_type=F32)


def _split3(x):
    hi = _bf(x)
    r1 = x - hi.astype(F32)
    mid = _bf(r1)
    lo = _bf(r1 - mid.astype(F32))
    return hi, mid, lo


def _ada_kernel(c_ref, w_ref, b_ref, o_ref):
    c = c_ref[...]
    ca = c * jax.nn.sigmoid(c)
    o_ref[...] = jnp.sum(ca * w_ref[...], axis=0, keepdims=True) + b_ref[...]


def _ada(c_col, w, b):
    d, n = w.shape
    tn = 1024
    return pl.pallas_call(
        _ada_kernel,
        out_shape=jax.ShapeDtypeStruct((1, n), F32),
        grid=(n // tn,),
        in_specs=[_full((d, 1)), pl.BlockSpec((d, tn), lambda j: (0, j)), pl.BlockSpec((1, tn), lambda j: (0, j))],
        out_specs=pl.BlockSpec((1, tn), lambda j: (0, j)),
        compiler_params=_cp("parallel"),
        name="ada_mod",
    )(c_col, w, b.reshape(1, n))


def _group_sumsq(blk, bd):
    sq = blk * blk
    hi = _bf(sq)
    lo = _bf(sq - hi.astype(F32))
    return _dot(hi, bd) + _dot(lo, bd)


def _rope(blk, cos, sin_a, sin_b):
    return blk * cos + pltpu.roll(blk, LANES - 16, 1) * sin_a + pltpu.roll(blk, 16, 1) * sin_b


def _modulate(x, scale, shift):
    return x * (1.0 + scale) + shift


def _ones_rows(n):
    r = lax.broadcasted_iota(jnp.int32, (ONES_ROWS, n), 0)
    return jnp.where(r == 0, 1.0, 0.0).astype(BF16)


def _inproj_gqa_kernel(x_ref, sc_ref, sh_ref, w_ref, qg_ref, kg_ref, cos_ref, sa_ref, sb_ref, bd_ref,
                       qT_ref, k_ref, vT_ref):
    u = _bf(_modulate(x_ref[...], sc_ref[...], sh_ref[...]))
    y = _dot(u, w_ref[...])
    cos, sa, sb, bd = cos_ref[...], sa_ref[...], sb_ref[...], bd_ref[...]
    nq = GQA_HEADS * HEAD64 // LANES
    nk = GQA_KV * HEAD64 // LANES
    scale = HEAD64 ** -0.5 * LOG2E
    for j in range(nq):
        blk = y[:, j * LANES:(j + 1) * LANES]
        ss = _group_sumsq(blk, bd)
        blk = blk * lax.rsqrt(ss * (1.0 / HEAD64) + RMS_EPS) * qg_ref[...]
        blk = _rope(blk, cos, sa, sb) * scale
        qT_ref[j * LANES:(j + 1) * LANES, :] = _bf(blk.T)
    for j in range(nk):
        blk = y[:, (nq + j) * LANES:(nq + j + 1) * LANES]
        ss = _group_sumsq(blk, bd)
        blk = blk * lax.rsqrt(ss * (1.0 / HEAD64) + RMS_EPS) * kg_ref[...]
        blk = _bf(_rope(blk, cos, sa, sb))
        k_ref[2 * j] = blk[:, :HEAD64]
        k_ref[2 * j + 1] = blk[:, HEAD64:]
    vT = _bf(y[:, (nq + nk) * LANES:].T)
    ones = _ones_rows(vT.shape[1])
    dve = HEAD64 + ONES_ROWS
    for h in range(GQA_KV):
        vT_ref[h * dve:h * dve + HEAD64, :] = vT[h * HEAD64:(h + 1) * HEAD64, :]
        vT_ref[h * dve + HEAD64:(h + 1) * dve, :] = ones


def _inproj_gqa(x, sc, sh, w, qg, kg, cos, sa, sb, bd, tm=256):
    s, d = x.shape
    n = w.shape[1]
    row = lambda i: (i, 0)
    return pl.pallas_call(
        _inproj_gqa_kernel,
        out_shape=(jax.ShapeDtypeStruct((GQA_HEADS * HEAD64, s), BF16),
                   jax.ShapeDtypeStruct((GQA_KV, s, HEAD64), BF16),
                   jax.ShapeDtypeStruct((GQA_KV * (HEAD64 + ONES_ROWS), s), BF16)),
        grid=(s // tm,),
        in_specs=[pl.BlockSpec((tm, d), row), _full((1, d)), _full((1, d)), _full((d, n)),
                  _full((1, LANES)), _full((1, LANES)),
                  pl.BlockSpec((tm, LANES), row), pl.BlockSpec((tm, LANES), row), pl.BlockSpec((tm, LANES), row),
                  _full((LANES, LANES))],
        out_specs=(pl.BlockSpec((GQA_HEADS * HEAD64, tm), lambda i: (0, i)),
                   pl.BlockSpec((GQA_KV, tm, HEAD64), lambda i: (0, i, 0)),
                   pl.BlockSpec((GQA_KV * (HEAD64 + ONES_ROWS), tm), lambda i: (0, i))),
        compiler_params=_cp("parallel"),
        name="inproj_gqa",
    )(x, sc, sh, w, qg, kg, cos, sa, sb, bd)


def _inproj_mla_kernel(x_ref, sc_ref, sh_ref, w_ref, qn_ref, kvn_ref, wuq_ref, wukv_ref, cos_ref, sa_ref, sb_ref,
                       qT_ref, k_ref, vT_ref):
    u = _bf(_modulate(x_ref[...], sc_ref[...], sh_ref[...]))
    y = _dot(u, w_ref[...])
    cos, sa, sb = cos_ref[...], sa_ref[...], sb_ref[...]
    cq = y[:, :MLA_QR]
    ckv = y[:, MLA_QR:MLA_QR + MLA_KVR]
    kr = y[:, MLA_QR + MLA_KVR:]
    cqn = cq * lax.rsqrt(jnp.mean(cq * cq, axis=-1, keepdims=True) + RMS_EPS) * qn_ref[...]
    ckvn = ckv * lax.rsqrt(jnp.mean(ckv * ckv, axis=-1, keepdims=True) + RMS_EPS) * kvn_ref[...]
    qm = _dot(_bf(cqn), wuq_ref[...])
    kvm = _dot(_bf(ckvn), wukv_ref[...])
    scale = (MLA_NOPE + HEAD64) ** -0.5 * LOG2E
    dq = MLA_NOPE + HEAD64
    dve = MLA_V + ONES_ROWS
    ones = _ones_rows(y.shape[0])
    k_rope = _bf(_rope(kr, cos, sa, sb)[:, :HEAD64])
    for p in range(MLA_HEADS // 2):
        rblk = qm[:, MLA_HEADS * MLA_NOPE + p * LANES:MLA_HEADS * MLA_NOPE + (p + 1) * LANES]
        rT = _bf((_rope(rblk, cos, sa, sb) * scale).T)
        for e in range(2):
            h = 2 * p + e
            qT_ref[h * dq + MLA_NOPE:(h + 1) * dq, :] = rT[e * HEAD64:(e + 1) * HEAD64, :]
    for h in range(MLA_HEADS):
        nope = qm[:, h * MLA_NOPE:(h + 1) * MLA_NOPE] * scale
        qT_ref[h * dq:h * dq + MLA_NOPE, :] = _bf(nope.T)
        k_ref[h, :, :MLA_NOPE] = _bf(kvm[:, h * 2 * LANES:h * 2 * LANES + MLA_NOPE])
        k_ref[h, :, MLA_NOPE:] = k_rope
        vT_ref[h * dve:h * dve + MLA_V, :] = _bf(kvm[:, h * 2 * LANES + MLA_NOPE:(h + 1) * 2 * LANES].T)
        vT_ref[h * dve + MLA_V:(h + 1) * dve, :] = ones


def _inproj_mla(x, sc, sh, w, qn, kvn, wuq, wukv, cos, sa, sb, tm=256):
    s, d = x.shape
    n = w.shape[1]
    dq = MLA_NOPE + HEAD64
    row = lambda i: (i, 0)
    return pl.pallas_call(
        _inproj_mla_kernel,
        out_shape=(jax.ShapeDtypeStruct((MLA_HEADS * dq, s), BF16),
                   jax.ShapeDtypeStruct((MLA_HEADS, s, dq), BF16),
                   jax.ShapeDtypeStruct((MLA_HEADS * (MLA_V + ONES_ROWS), s), BF16)),
        grid=(s // tm,),
        in_specs=[pl.BlockSpec((tm, d), row), _full((1, d)), _full((1, d)), _full((d, n)),
                  _full((1, MLA_QR)), _full((1, MLA_KVR)), _full(wuq.shape), _full(wukv.shape),
                  pl.BlockSpec((tm, LANES), row), pl.BlockSpec((tm, LANES), row), pl.BlockSpec((tm, LANES), row)],
        out_specs=(pl.BlockSpec((MLA_HEADS * dq, tm), lambda i: (0, i)),
                   pl.BlockSpec((MLA_HEADS, tm, dq), lambda i: (0, i, 0)),
                   pl.BlockSpec((MLA_HEADS * (MLA_V + ONES_ROWS), tm), lambda i: (0, i))),
        compiler_params=_cp("parallel"),
        name="inproj_mla",
    )(x, sc, sh, w, qn, kvn, wuq, wukv, cos, sa, sb)


def _inproj_mlstm_kernel(x_ref, xp_ref, xn_ref, sc_ref, sh_ref, w_ref, conv_ref, gb_ref,
                         q_ref, kT_ref, v_ref, o_ref, gT_ref):
    i = pl.program_id(0)
    last = pl.num_programs(0) - 1
    sc, sh = sc_ref[...], sh_ref[...]
    u = _bf(_modulate(x_ref[...], sc, sh))
    y = _dot(u, w_ref[...])
    tm = y.shape[0]
    qk = y[:, :2 * ML_W]
    halo = jnp.concatenate([xp_ref[...], xn_ref[...]], axis=0)
    hy = _dot(_bf(_modulate(halo, sc, sh)), w_ref[:, :2 * ML_W])
    prev_row = jnp.where(i == 0, 0.0, hy[7:8, :])
    next_row = jnp.where(i == last, 0.0, hy[8:9, :])
    rows = lax.broadcasted_iota(jnp.int32, qk.shape, 0)
    prev = jnp.where(rows == 0, prev_row, pltpu.roll(qk, 1, 0))
    nxt = jnp.where(rows == tm - 1, next_row, pltpu.roll(qk, tm - 1, 0))
    cw = conv_ref[...]
    z = cw[0:1, :] * prev + cw[1:2, :] * qk + cw[2:3, :] * nxt
    z = z * jax.nn.sigmoid(z)
    q_ref[...] = _bf(z[:, :ML_W])
    kT_ref[...] = (z[:, ML_W:] * (ML_D ** -0.5)).T
    v_ref[...] = _bf(y[:, 2 * ML_W:3 * ML_W])
    o_ref[...] = y[:, 3 * ML_W:4 * ML_W]
    g = y[:, 4 * ML_W:] + gb_ref[...]
    gT_ref[...] = g.T[:4 * ML_HEADS, :]


def _inproj_mlstm(x, sc, sh, w, conv, gb, tm=256):
    s, d = x.shape
    n = w.shape[1]
    row = lambda i: (i, 0)
    col = lambda i: (0, i)
    r8 = tm // 8
    return pl.pallas_call(
        _inproj_mlstm_kernel,
        out_shape=(jax.ShapeDtypeStruct((s, ML_W), BF16), jax.ShapeDtypeStruct((ML_W, s), F32),
                   jax.ShapeDtypeStruct((s, ML_W), BF16), jax.ShapeDtypeStruct((s, ML_W), F32),
                   jax.ShapeDtypeStruct((4 * ML_HEADS, s), F32)),
        grid=(s // tm,),
        in_specs=[pl.BlockSpec((tm, d), row),
                  pl.BlockSpec((8, d), lambda i: (jnp.maximum(i * r8 - 1, 0), 0)),
                  pl.BlockSpec((8, d), lambda i: (jnp.minimum((i + 1) * r8, s // 8 - 1), 0)),
                  _full((1, d)), _full((1, d)), _full((d, n)), _full((3, 2 * ML_W)), _full((1, LANES))],
        out_specs=(pl.BlockSpec((tm, ML_W), row), pl.BlockSpec((ML_W, tm), col),
                   pl.BlockSpec((tm, ML_W), row), pl.BlockSpec((tm, ML_W), row),
                   pl.BlockSpec((4 * ML_HEADS, tm), col)),
        compiler_params=_cp("parallel"),
        name="inproj_mlstm",
    )(x, x, x, sc, sh, w, conv, gb)


def _attn_kernel(qT_ref, k_ref, vT_ref, o_ref, s_sc, acc_sc, *, groups, dq, dv, tk):
    hp = k_ref.shape[0]
    seq = k_ref.shape[1]
    n = seq // tk
    dve = dv + ONES_ROWS
    cq = qT_ref.shape[1]
    nq = groups * cq
    qTs = []
    for hh in range(hp):
        r0 = hh * groups * dq
        qTs.append(jnp.concatenate([qT_ref[r0 + g * dq:r0 + (g + 1) * dq, :] for g in range(groups)], axis=1)
                   if groups > 1 else qT_ref[r0:r0 + dq, :])
    acc_sc[...] = jnp.zeros(acc_sc.shape, F32)

    def scores(hh, c, buf):
        off = c * tk if isinstance(c, int) else pl.multiple_of(c * tk, tk)
        s = _dot(k_ref[hh, pl.ds(off, tk), :], qTs[hh])
        s_sc[hh, buf] = s
        return jnp.max(s, axis=0, keepdims=True)

    def accumulate(hh, c, buf, mx, m_old):
        off = c * tk if isinstance(c, int) else pl.multiple_of(c * tk, tk)
        m_new = jnp.maximum(m_old, mx)
        p = jnp.exp2(_bf(s_sc[hh, buf] - m_new))
        pv = _dot(vT_ref[hh * dve:(hh + 1) * dve, pl.ds(off, tk)], p)
        acc_sc[hh] = jnp.exp2(m_old - m_new) * acc_sc[hh] + pv
        return m_new

    def body(base, carry, tail=False):
        mxs, ms = [list(t) for t in carry[0]], list(carry[1])
        for j in range(ATTN_BUFS):
            for hh in range(hp):
                if not (tail and j + ATTN_AHEAD >= ATTN_BUFS):
                    mxs[hh].append(scores(hh, base + j + ATTN_AHEAD, (j + ATTN_AHEAD) % ATTN_BUFS))
                ms[hh] = accumulate(hh, base + j, j, mxs[hh].pop(0), ms[hh])
        return tuple(tuple(t) for t in mxs), tuple(ms)

    carry = (tuple(tuple(scores(hh, c, c) for c in range(ATTN_AHEAD)) for hh in range(hp)),
             tuple(jnp.full((1, nq), NEG_BIG, F32) for _ in range(hp)))
    carry = lax.fori_loop(0, n // ATTN_BUFS - 1, lambda i, cr: body(i * ATTN_BUFS, cr), carry)
    body(n - ATTN_BUFS, carry, tail=True)
    for hh in range(hp):
        oT = acc_sc[hh, :dv, :] * (1.0 / acc_sc[hh, dv:dv + 1, :])
        if groups > 1:
            oT = jnp.concatenate([oT[:, g * cq:(g + 1) * cq] for g in range(groups)], axis=0)
        o_ref[:, hh * groups * dv:(hh + 1) * groups * dv] = _bf(oT.T)


def _attention(qT, k, vT, *, groups, dq, dv, cq, hp, tk=512):
    h, s, _ = k.shape
    nq = groups * cq
    dve = dv + ONES_ROWS
    assert (s // tk) % ATTN_BUFS == 0 and h % hp == 0
    kern = functools.partial(_attn_kernel, groups=groups, dq=dq, dv=dv, tk=tk)
    return pl.pallas_call(
        kern,
        out_shape=jax.ShapeDtypeStruct((s, h * groups * dv), BF16),
        grid=(h // hp, s // cq),
        in_specs=[pl.BlockSpec((hp * groups * dq, cq), lambda a, i: (a, i)),
                  pl.BlockSpec((hp, s, dq), lambda a, i: (a, 0, 0)),
                  pl.BlockSpec((hp * dve, s), lambda a, i: (a, 0))],
        out_specs=pl.BlockSpec((cq, hp * groups * dv), lambda a, i: (i, a)),
        scratch_shapes=[pltpu.VMEM((hp, ATTN_BUFS, tk, nq), F32), pltpu.VMEM((hp, dve, nq), F32)],
        compiler_params=_cp("parallel", "parallel"),
        name="attention_g%d" % groups,
    )(qT, k, vT)


def _mlstm_kernel(q_ref, kT_ref, v_ref, o_ref, g_ref, hn_ref, out_ref,
                  cn_sc, mprev_sc, w_sc, r_sc, lf_sc, g_sc, mloc_sc, hacc_sc):
    head = pl.program_id(0)
    nc = g_ref.shape[1]
    L = CHUNK
    ii = lax.broadcasted_iota(jnp.int32, (L, L), 0)
    jj = lax.broadcasted_iota(jnp.int32, (L, L), 1)
    e0 = jnp.where(lax.broadcasted_iota(jnp.int32, (L, LANES), 1) == 0, 1.0, 0.0).astype(BF16)

    for direction in range(2):
        fw = direction == 0
        mask = (jj <= ii) if fw else (jj >= ii)
        tri = jnp.where((ii <= jj) if fw else (ii >= jj), 1.0, 0.0).astype(BF16)
        li = g_ref[2 * ML_HEADS * direction + head]
        fg = g_ref[2 * ML_HEADS * direction + ML_HEADS + head]
        lf = jnp.minimum(fg, 0.0) - jnp.log1p(jnp.exp(-jnp.abs(fg)))
        hi, mid, lo = _split3(lf)
        b = _dot(hi, tri) + _dot(mid, tri) + _dot(lo, tri)
        gtot = jnp.sum(lf, axis=-1, keepdims=True)
        a = gtot - b + li
        mloc = jnp.max(a, axis=-1, keepdims=True)
        w_sc[...] = jnp.exp(a - mloc)
        r_sc[...] = li - b
        lf_sc[...] = lf
        g_sc[...] = jnp.broadcast_to(gtot, (nc, LANES))
        mloc_sc[...] = jnp.broadcast_to(mloc, (nc, LANES))

        def scan_body(i, carry, fw=fw):
            cn, m = carry
            cc = i if fw else nc - 1 - i
            off = pl.multiple_of(cc * L, L)
            cn_sc[cc] = _bf(cn)
            mprev_sc[pl.ds(cc, 1), :] = jnp.broadcast_to(m, (1, LANES))
            kw = _bf(kT_ref[:, pl.ds(off, L)] * w_sc[pl.ds(cc, 1), :])
            vext = jnp.concatenate([v_ref[pl.ds(off, L), :], e0], axis=1)
            kv = _dot(kw, vext)
            g = g_sc[pl.ds(cc, 1), 0:1]
            ml = mloc_sc[pl.ds(cc, 1), 0:1]
            m_new = jnp.maximum(g + m, ml)
            cn = jnp.exp(g + m - m_new) * cn + jnp.exp(ml - m_new) * kv
            return cn, m_new

        lax.fori_loop(0, nc, scan_body, (jnp.zeros((ML_D, 2 * ML_D), F32), jnp.zeros((1, 1), F32)), unroll=2)

        def out_body(cc, carry, fw=fw, mask=mask):
            off = pl.multiple_of(cc * L, L)
            mp = mprev_sc[pl.ds(cc, 1), 0:1]
            rm = jnp.where(mask, r_sc[pl.ds(cc, 1), :], -jnp.inf)
            mt = jnp.maximum(mp, jnp.max(rm, axis=-1, keepdims=True))
            d = jnp.exp(rm - mt)
            iw = jnp.exp(mp - mt)
            qc = q_ref[pl.ds(off, L), :]
            qk = _dot(qc, _bf(kT_ref[:, pl.ds(off, L)])) * d
            num_i = _dot(_bf(qk), v_ref[pl.ds(off, L), :])
            den_i = jnp.sum(qk, axis=-1, keepdims=True)
            inter = _dot(qc, cn_sc[cc])
            num = num_i + iw * inter[:, :ML_D]
            den = den_i + iw * inter[:, ML_D:ML_D + 1]
            bcol = jnp.sum(jnp.where(mask, lf_sc[pl.ds(cc, 1), :], 0.0), axis=-1, keepdims=True)
            h = num / jnp.maximum(jnp.abs(den), jnp.exp(-(bcol + mt)))
            if fw:
                hacc_sc[pl.ds(off, L), :] = h
            else:
                h = h + hacc_sc[pl.ds(off, L), :]
                mu = jnp.mean(h, axis=-1, keepdims=True)
                hc = h - mu
                var = jnp.mean(hc * hc, axis=-1, keepdims=True)
                hn = hc * lax.rsqrt(var + LN_EPS) * hn_ref[...]
                out_ref[pl.ds(off, L), :] = _bf(jax.nn.sigmoid(o_ref[pl.ds(off, L), :]) * hn)
            return carry

        lax.fori_loop(0, nc, out_body, 0, unroll=4)


def _mlstm(q, kT, v, o, g3, hn):
    s = q.shape[0]
    nc = s // CHUNK
    colblk = lambda h: (0, h)
    return pl.pallas_call(
        _mlstm_kernel,
        out_shape=jax.ShapeDtypeStruct((s, ML_W), BF16),
        grid=(ML_HEADS,),
        in_specs=[pl.BlockSpec((s, ML_D), colblk), pl.BlockSpec((ML_D, s), lambda h: (h, 0)),
                  pl.BlockSpec((s, ML_D), colblk), pl.BlockSpec((s, ML_D), colblk),
                  _full(g3.shape), pl.BlockSpec((1, ML_D), colblk)],
        out_specs=pl.BlockSpec((s, ML_D), colblk),
        scratch_shapes=[pltpu.VMEM((nc, ML_D, 2 * ML_D), BF16), pltpu.VMEM((nc, LANES), F32),
                        pltpu.VMEM((nc, CHUNK), F32), pltpu.VMEM((nc, CHUNK), F32), pltpu.VMEM((nc, CHUNK), F32),
                        pltpu.VMEM((nc, LANES), F32), pltpu.VMEM((nc, LANES), F32),
                        pltpu.VMEM((s, ML_D), F32)],
        compiler_params=_cp("parallel"),
        name="mlstm",
    )(q, kT, v, o, g3, hn)


def _layer_norm(z, g, b):
    mu = jnp.mean(z, axis=-1, keepdims=True)
    zc = z - mu
    var = jnp.mean(zc * zc, axis=-1, keepdims=True)
    return zc * lax.rsqrt(var + LN_EPS) * g + b


def _outproj_kernel(*refs, route):
    if route:
        (x_ref, a_ref, b_ref, c_ref, w_ref, gate_ref, g_ref, bb_ref, sc_ref, sh_ref, rw_both_ref, rw_high_ref,
         x1_ref, u2_ref, lg_ref) = refs
    else:
        x_ref, a_ref, b_ref, c_ref, w_ref, gate_ref, g_ref, bb_ref, sc_ref, sh_ref, x1_ref, u2_ref = refs
    na, nb = a_ref.shape[1], b_ref.shape[1]
    y = _dot(a_ref[...], w_ref[:na, :]) + _dot(b_ref[...], w_ref[na:na + nb, :]) + _dot(c_ref[...], w_ref[na + nb:, :])
    x1 = _layer_norm(ALPHA * x_ref[...] + gate_ref[...] * y, g_ref[...], bb_ref[...])
    x1_ref[...] = x1
    u2 = _modulate(x1, sc_ref[...], sh_ref[...])
    u2_ref[...] = u2.astype(u2_ref.dtype)
    if route:
        hi = _bf(u2)
        lo = _bf(u2 - hi.astype(F32))
        both = _dot(hi, rw_both_ref[...])
        lg_ref[...] = both[:, :LANES] + (_dot(lo, rw_high_ref[...]) + both[:, LANES:])


def _outproj(x, a, b, c, w, gate, g, bb, sc, sh, rw=None, tm=512):
    s, d = x.shape
    row = lambda i: (i, 0)
    route = rw is not None
    ins = [x, a, b, c, w, gate, g, bb, sc, sh]
    in_specs = [pl.BlockSpec((tm, d), row), pl.BlockSpec((tm, a.shape[1]), row), pl.BlockSpec((tm, b.shape[1]), row),
                pl.BlockSpec((tm, c.shape[1]), row), _full(w.shape)] + [_full((1, d))] * 5
    out_shape = [jax.ShapeDtypeStruct((s, d), F32), jax.ShapeDtypeStruct((s, d), F32 if route else BF16)]
    out_specs = [pl.BlockSpec((tm, d), row), pl.BlockSpec((tm, d), row)]
    if route:
        rwp = jnp.zeros((d, LANES), F32).at[:, :rw.shape[1]].set(rw)
        rwh = _bf(rwp)
        rwl = _bf(rwp - rwh.astype(F32))
        ins += [jnp.concatenate([rwh, rwl], axis=1), rwh]
        in_specs += [_full((d, 2 * LANES)), _full((d, LANES))]
        out_shape.append(jax.ShapeDtypeStruct((s, LANES), F32))
        out_specs.append(pl.BlockSpec((tm, LANES), row))
    return pl.pallas_call(
        functools.partial(_outproj_kernel, route=route),
        out_shape=tuple(out_shape),
        grid=(s // tm,),
        in_specs=in_specs,
        out_specs=tuple(out_specs),
        compiler_params=_cp("parallel"),
        name="outproj_ln1",
    )(*ins)


def _swiglu_partial(x, wg, wu, wd):
    gt = _dot(x, _bf(wg))
    up = _dot(x, _bf(wu))
    return _dot(_bf(gt * jax.nn.sigmoid(gt) * up), _bf(wd))


def _ffn_dense_kernel(x_ref, wg_ref, wu_ref, wd_ref, y_ref):
    j = pl.program_id(1)

    @pl.when(j == 0)
    def _():
        y_ref[...] = _swiglu_partial(x_ref[...], wg_ref[...], wu_ref[...], wd_ref[...])

    @pl.when(j > 0)
    def _():
        y_ref[...] += _swiglu_partial(x_ref[...], wg_ref[...], wu_ref[...], wd_ref[...])


def _ffn_dense(x, wg, wu, wd, tm=FFN_TM, tf=256):
    s, d = x.shape
    f = wg.shape[1]
    return pl.pallas_call(
        _ffn_dense_kernel,
        out_shape=jax.ShapeDtypeStruct((s, d), F32),
        grid=(s // tm, f // tf),
        in_specs=[pl.BlockSpec((tm, d), lambda i, j: (i, 0)), pl.BlockSpec((d, tf), lambda i, j: (0, j)),
                  pl.BlockSpec((d, tf), lambda i, j: (0, j)), pl.BlockSpec((tf, d), lambda i, j: (j, 0))],
        out_specs=pl.BlockSpec((tm, d), lambda i, j: (i, 0)),
        compiler_params=_cp("parallel", "arbitrary"),
        name="swiglu_dense",
    )(x, wg, wu, wd)


def _ffn_routed_kernel(te_ref, nu_ref, src_ref, u_hbm, wg_ref, wu_ref, wd_ref, y_ref, xf_buf, xb_buf, sem):
    i, j = pl.program_id(0), pl.program_id(1)
    n_live = nu_ref[0]
    live = i < n_live
    tm = y_ref.shape[0]

    def row_copy(tile, r):
        return pltpu.make_async_copy(u_hbm.at[pl.ds(src_ref[tile * tm + r], 1)], xf_buf.at[pl.ds(r, 1)], sem)

    def issue(tile, first, count):
        def body(r, carry):
            row_copy(tile, first + r).start()
            return carry
        lax.fori_loop(0, count, body, 0)

    @pl.when((i == 0) & (j == 0))
    def _():
        issue(0, 0, tm)

    @pl.when(live & (j == 0))
    def _():
        pltpu.make_async_copy(u_hbm.at[pl.ds(0, tm)], xf_buf, sem).wait()
        xb_buf[...] = _bf(xf_buf[...])
        y_ref[...] = _swiglu_partial(xb_buf[...], wg_ref[0], wu_ref[0], wd_ref[0])

    @pl.when((j >= 1) & (j <= tm // GATHER_ROWS_PER_STEP) & (i + 1 < n_live))
    def _():
        issue(i + 1, (j - 1) * GATHER_ROWS_PER_STEP, GATHER_ROWS_PER_STEP)

    @pl.when(live & (j > 0))
    def _():
        y_ref[...] += _swiglu_partial(xb_buf[...], wg_ref[0], wu_ref[0], wd_ref[0])

    @pl.when(jnp.logical_not(live) & (j == 0))
    def _():
        y_ref[...] = jnp.zeros(y_ref.shape, F32)


def _ffn_routed(u, src, wg, wu, wd, tile_expert, n_used, tm, tf=256):
    s, d = u.shape
    f = wg.shape[2]
    nt, nf = src.shape[0] // tm, f // tf
    assert tm % GATHER_ROWS_PER_STEP == 0 and nf > tm // GATHER_ROWS_PER_STEP

    def clamp(i, j, nu):
        live = i < nu[0]
        return jnp.where(live, i, nu[0] - 1), jnp.where(live, j, nf - 1)

    def gmap(i, j, te, nu, sr):
        ii, jj = clamp(i, j, nu)
        return te[ii], 0, jj

    def dmap(i, j, te, nu, sr):
        ii, jj = clamp(i, j, nu)
        return te[ii], jj, 0

    return pl.pallas_call(
        _ffn_routed_kernel,
        out_shape=jax.ShapeDtypeStruct((nt * tm, d), F32),
        grid_spec=pltpu.PrefetchScalarGridSpec(
            num_scalar_prefetch=3,
            grid=(nt, nf),
            in_specs=[pl.BlockSpec(memory_space=pl.ANY), pl.BlockSpec((1, d, tf), gmap), pl.BlockSpec((1, d, tf), gmap),
                      pl.BlockSpec((1, tf, d), dmap)],
            out_specs=pl.BlockSpec((tm, d), lambda i, j, te, nu, sr: (i, 0)),
            scratch_shapes=[pltpu.VMEM((tm, d), F32), pltpu.VMEM((tm, d), BF16), pltpu.SemaphoreType.DMA(())]),
        compiler_params=_cp("arbitrary", "arbitrary"),
        name="swiglu_routed",
    )(tile_expert, n_used, src, u, wg, wu, wd)


def _ln2_kernel(x_ref, f_ref, gate_ref, g_ref, b_ref, o_ref):
    o_ref[...] = _layer_norm(ALPHA * x_ref[...] + gate_ref[...] * f_ref[...], g_ref[...], b_ref[...])


def _ln2(x, f, gate, g, b, tm=512):
    s, d = x.shape
    row = lambda i: (i, 0)
    return pl.pallas_call(
        _ln2_kernel,
        out_shape=jax.ShapeDtypeStruct((s, d), F32),
        grid=(s // tm,),
        in_specs=[pl.BlockSpec((tm, d), row), pl.BlockSpec((tm, d), row)] + [_full((1, d))] * 3,
        out_specs=pl.BlockSpec((tm, d), row),
        compiler_params=_cp("parallel"),
        name="residual_ln2",
    )(x, f, gate, g, b)


def _row_copy(y_hbm, idx_ref, buf, sem, tile, slot, r, tm):
    return pltpu.make_async_copy(y_hbm.at[pl.ds(idx_ref[tile * tm + r], 1)], buf.at[slot, pl.ds(r, 1)], sem.at[slot])


def _ln2_combine_kernel(d1_ref, d2_ref, x_ref, info_ref, y_hbm, gate_ref, g_ref, b_ref, o_ref, a_buf, b_buf, sem):
    i = pl.program_id(0)
    n = pl.num_programs(0)
    tm = x_ref.shape[0]

    def issue(tile, slot):
        def body(r, carry):
            _row_copy(y_hbm, d1_ref, a_buf, sem, tile, slot, r, tm).start()
            _row_copy(y_hbm, d2_ref, b_buf, sem, tile, slot, r, tm).start()
            return carry
        lax.fori_loop(0, tm, body, 0)

    @pl.when(i == 0)
    def _():
        issue(0, 0)

    @pl.when(i + 1 < n)
    def _():
        issue(i + 1, (i + 1) % 2)

    slot = i % 2
    pltpu.make_async_copy(y_hbm.at[pl.ds(0, tm)], a_buf.at[slot], sem.at[slot]).wait()
    pltpu.make_async_copy(y_hbm.at[pl.ds(0, tm)], b_buf.at[slot], sem.at[slot]).wait()
    info = info_ref[...]
    f = info[:, 2:3] * a_buf[slot] + info[:, 3:4] * b_buf[slot]
    o_ref[...] = _layer_norm(ALPHA * x_ref[...] + gate_ref[...] * f, g_ref[...], b_ref[...])


def _ln2_combine(x, info, y, d1, d2, gate, g, b, tm=256):
    s, d = x.shape
    row = lambda i, *_: (i, 0)
    one = lambda i, *_: (0, 0)
    return pl.pallas_call(
        _ln2_combine_kernel,
        out_shape=jax.ShapeDtypeStruct((s, d), F32),
        grid_spec=pltpu.PrefetchScalarGridSpec(
            num_scalar_prefetch=2,
            grid=(s // tm,),
            in_specs=[pl.BlockSpec((tm, d), row), pl.BlockSpec((tm, LANES), row), pl.BlockSpec(memory_space=pl.ANY),
                      pl.BlockSpec((1, d), one), pl.BlockSpec((1, d), one), pl.BlockSpec((1, d), one)],
            out_specs=pl.BlockSpec((tm, d), row),
            scratch_shapes=[pltpu.VMEM((2, tm, d), F32), pltpu.VMEM((2, tm, d), F32), pltpu.SemaphoreType.DMA((2,))]),
        compiler_params=_cp("arbitrary"),
        name="combine_ln2",
    )(d1, d2, x, info, y, gate, g, b)


def _router_kernel(lg_ref, info_ref, cnt_ref, carry_sc):
    i = pl.program_id(0)

    @pl.when(i == 0)
    def _():
        carry_sc[...] = jnp.zeros(carry_sc.shape, F32)

    lg = lg_ref[...]
    tm = lg.shape[0]
    lane = lax.broadcasted_iota(jnp.int32, lg.shape, 1)
    lg = jnp.where(lane < N_EXPERTS, lg, -jnp.inf)
    v1 = jnp.max(lg, axis=-1, keepdims=True)
    i1 = jnp.min(jnp.where(lg == v1, lane, LANES), axis=-1, keepdims=True)
    lg2 = jnp.where(lane == i1, -jnp.inf, lg)
    v2 = jnp.max(lg2, axis=-1, keepdims=True)
    i2 = jnp.min(jnp.where(lg2 == v2, lane, LANES), axis=-1, keepdims=True)
    e21 = jnp.exp(v2 - v1)
    w1 = 1.0 / (1.0 + e21)
    w2 = e21 * w1
    oh1 = lane == i1
    oh2 = lane == i2
    oh = jnp.where(oh1 | oh2, 1.0, 0.0)
    rr = lax.broadcasted_iota(jnp.int32, (tm, tm), 0)
    cc = lax.broadcasted_iota(jnp.int32, (tm, tm), 1)
    strict = jnp.where(cc < rr, 1.0, 0.0).astype(BF16)
    before = _dot(strict, _bf(oh)) + carry_sc[...]
    r1 = jnp.sum(jnp.where(oh1, before, 0.0), axis=-1, keepdims=True)
    r2 = jnp.sum(jnp.where(oh2, before, 0.0), axis=-1, keepdims=True)
    vals = (i1.astype(F32), i2.astype(F32), w1, w2, r1, r2)
    info = jnp.zeros(lg.shape, F32)
    for n, v in enumerate(vals):
        info = jnp.where(lane == n, v, info)
    info_ref[...] = info
    carry_sc[...] += jnp.sum(oh, axis=0, keepdims=True)
    cnt_ref[...] = carry_sc[...]


def _router(logits, tm=512):
    s = logits.shape[0]
    return pl.pallas_call(
        _router_kernel,
        out_shape=(jax.ShapeDtypeStruct((s, LANES), F32), jax.ShapeDtypeStruct((1, LANES), F32)),
        grid=(s // tm,),
        in_specs=[pl.BlockSpec((tm, LANES), lambda i: (i, 0))],
        out_specs=(pl.BlockSpec((tm, LANES), lambda i: (i, 0)), _full((1, LANES))),
        scratch_shapes=[pltpu.VMEM((1, LANES), F32)],
        compiler_params=_cp("arbitrary"),
        name="router_top2",
    )(logits)


def _rope_tables(s):
    rows = s // GRID_W
    n_freq = HEAD64 // 4
    inv_freq = ROPE_THETA ** (-jnp.arange(n_freq, dtype=F32) / n_freq)
    ang_r = jnp.arange(rows, dtype=F32)[:, None] * inv_freq
    ang_c = jnp.arange(GRID_W, dtype=F32)[:, None] * inv_freq

    def table(fn):
        tr = jnp.broadcast_to(fn(ang_r)[:, None, :], (rows, GRID_W, n_freq)).reshape(s, n_freq)
        tc = jnp.broadcast_to(fn(ang_c)[None, :, :], (rows, GRID_W, n_freq)).reshape(s, n_freq)
        return jnp.concatenate([tr, tr, tc, tc] * 2, axis=-1)

    cos, sin = table(jnp.cos), table(jnp.sin)
    first_half = (jnp.arange(LANES) % 32) < 16
    return cos, jnp.where(first_half, -sin, 0.0), jnp.where(first_half, 0.0, sin)


def _moe_ffn(u2, logits, wg, wu, wd, tm):
    s, d = u2.shape
    e = wg.shape[0]
    info, counts = _router(logits)
    e1, e2 = info[:, 0].astype(jnp.int32), info[:, 1].astype(jnp.int32)
    w1, w2 = info[:, 2], info[:, 3]
    r1, r2 = info[:, 4].astype(jnp.int32), info[:, 5].astype(jnp.int32)
    cnt = counts[0, :e].astype(jnp.int32)
    tiles = (cnt + tm - 1) // tm
    tile_end = jnp.cumsum(tiles)
    start = (tile_end - tiles) * tm
    nt = (2 * s) // tm + e
    n_used = tile_end[-1:].astype(jnp.int32)
    tile_expert = jnp.sum((jnp.arange(nt)[:, None] >= tile_end[None, :]).astype(jnp.int32), axis=1)
    tile_expert = jnp.minimum(tile_expert, e - 1)
    d1, d2 = start[e1] + r1, start[e2] + r2
    tok = jnp.arange(s, dtype=jnp.int32)
    src = jnp.zeros((nt * tm,), jnp.int32).at[d1].set(tok).at[d2].set(tok)
    y = _ffn_routed(u2, src, wg, wu, wd, tile_expert, n_used, tm=tm)
    return info, y, d1, d2


def _layer(x, c_col, tabs, p, moe):
    s, d = x.shape
    cos, sa, sb = tabs
    mod = _ada(c_col, p["ada_w"], p["ada_b"])
    sh1, sc1, g1, sh2, sc2, g2 = [mod[:, i * d:(i + 1) * d] for i in range(6)]

    w_in = p["w_in"]
    n_gqa = (GQA_HEADS + 2 * GQA_KV) * HEAD64
    n_mla = MLA_QR + MLA_KVR + HEAD64
    w_gqa = _bf(w_in[:, :n_gqa])
    w_mla = _bf(jnp.pad(w_in[:, n_gqa:n_gqa + n_mla], ((0, 0), (0, LANES - HEAD64))))
    w_ml = _bf(jnp.pad(w_in[:, n_gqa + n_mla:], ((0, 0), (0, LANES - 4 * ML_HEADS))))

    ii = jnp.arange(LANES)
    bd = _bf((ii[:, None] // HEAD64 == ii[None, :] // HEAD64).astype(F32))
    qg = jnp.tile(p["gqa_q_norm"], 2).reshape(1, LANES)
    kg = jnp.tile(p["gqa_k_norm"], 2).reshape(1, LANES)
    qT, k, vT = _inproj_gqa(x, sc1, sh1, w_gqa, qg, kg, cos, sa, sb, bd)
    out_a = _attention(qT, k, vT, groups=GQA_HEADS // GQA_KV, dq=HEAD64, dv=HEAD64, cq=128, hp=4)

    dq = MLA_NOPE + HEAD64
    wuq = p["mla_w_uq"].reshape(MLA_QR, MLA_HEADS, dq)
    wuq = _bf(jnp.concatenate([wuq[:, :, :MLA_NOPE].reshape(MLA_QR, -1), wuq[:, :, MLA_NOPE:].reshape(MLA_QR, -1)], axis=1))
    qTm, km, vTm = _inproj_mla(x, sc1, sh1, w_mla, p["mla_q_norm"].reshape(1, -1), p["mla_kv_norm"].reshape(1, -1),
                               wuq, _bf(p["mla_w_ukv"]), cos, sa, sb)
    out_b = _attention(qTm, km, vTm, groups=1, dq=dq, dv=MLA_V, cq=512, hp=2)

    gb = jnp.pad(p["mlstm_gate_b"], (0, LANES - 4 * ML_HEADS)).reshape(1, LANES)
    mq, mkT, mv, mo, gT = _inproj_mlstm(x, sc1, sh1, w_ml, p["mlstm_conv"], gb)
    out_c = _mlstm(mq, mkT, mv, mo, gT.reshape(4 * ML_HEADS, s // CHUNK, CHUNK), p["mlstm_head_norm"].reshape(1, -1))

    ln = [p[n].reshape(1, d) for n in ("ln1_g", "ln1_b", "ln2_g", "ln2_b")]
    if moe:
        x1, u2, logits = _outproj(x, out_a, out_b, out_c, _bf(p["w_out"]), g1, ln[0], ln[1], sc2, sh2, rw=p["router_w"])
        info, y, d1, d2 = _moe_ffn(u2, logits, p["moe_w_gate"], p["moe_w_up"], p["moe_w_down"], tm=FFN_TM)
        return _ln2_combine(x1, info, y, d1, d2, g2, ln[2], ln[3])
    else:
        x1, u2 = _outproj(x, out_a, out_b, out_c, _bf(p["w_out"]), g1, ln[0], ln[1], sc2, sh2)
        f = _ffn_dense(u2, p["ffn_w_gate"], p["ffn_w_up"], p["ffn_w_down"])
    return _ln2(x1, f, g2, ln[2], ln[3])


_NAMES = ("ada_w", "ada_b", "w_in", "gqa_q_norm", "gqa_k_norm", "mla_q_norm", "mla_w_uq", "mla_kv_norm", "mla_w_ukv",
          "mlstm_conv", "mlstm_gate_b", "mlstm_head_norm", "w_out", "ln1_g", "ln1_b")


def kernel(x, c, l0_ada_w, l0_ada_b, l0_w_in, l0_gqa_q_norm, l0_gqa_k_norm, l0_mla_q_norm, l0_mla_w_uq, l0_mla_kv_norm, l0_mla_w_ukv, l0_mlstm_conv, l0_mlstm_gate_b, l0_mlstm_head_norm, l0_w_out, l0_ln1_g, l0_ln1_b, l0_ffn_w_gate, l0_ffn_w_up, l0_ffn_w_down, l0_ln2_g, l0_ln2_b, l1_ada_w, l1_ada_b, l1_w_in, l1_gqa_q_norm, l1_gqa_k_norm, l1_mla_q_norm, l1_mla_w_uq, l1_mla_kv_norm, l1_mla_w_ukv, l1_mlstm_conv, l1_mlstm_gate_b, l1_mlstm_head_norm, l1_w_out, l1_ln1_g, l1_ln1_b, l1_router_w, l1_moe_w_gate, l1_moe_w_up, l1_moe_w_down, l1_ln2_g, l1_ln2_b):
    b, s, d = x.shape
    p0 = dict(zip(_NAMES, (l0_ada_w, l0_ada_b, l0_w_in, l0_gqa_q_norm, l0_gqa_k_norm, l0_mla_q_norm, l0_mla_w_uq,
                           l0_mla_kv_norm, l0_mla_w_ukv, l0_mlstm_conv, l0_mlstm_gate_b, l0_mlstm_head_norm, l0_w_out,
                           l0_ln1_g, l0_ln1_b)))
    p0.update(ffn_w_gate=l0_ffn_w_gate, ffn_w_up=l0_ffn_w_up, ffn_w_down=l0_ffn_w_down, ln2_g=l0_ln2_g, ln2_b=l0_ln2_b)
    p1 = dict(zip(_NAMES, (l1_ada_w, l1_ada_b, l1_w_in, l1_gqa_q_norm, l1_gqa_k_norm, l1_mla_q_norm, l1_mla_w_uq,
                           l1_mla_kv_norm, l1_mla_w_ukv, l1_mlstm_conv, l1_mlstm_gate_b, l1_mlstm_head_norm, l1_w_out,
                           l1_ln1_g, l1_ln1_b)))
    p1.update(router_w=l1_router_w, moe_w_gate=l1_moe_w_gate, moe_w_up=l1_moe_w_up, moe_w_down=l1_moe_w_down,
              ln2_g=l1_ln2_g, ln2_b=l1_ln2_b)
    tabs = _rope_tables(s)
    outs = []
    for bi in range(b):
        xb = x[bi]
        c_col = c[bi].reshape(d, 1)
        xb = _layer(xb, c_col, tabs, p0, moe=False)
        xb = _layer(xb, c_col, tabs, p1, moe=True)
        outs.append(xb)
    return jnp.stack(outs, axis=0)
```

```python
import functools

import jax
import jax.numpy as jnp
from jax import lax
from jax.experimental import pallas as pl
from jax.experimental.pallas import tpu as pltpu

F32 = jnp.float32
BF16 = jnp.bfloat16

GRID_W = 64
ROPE_THETA = 10000.0
HEAD64 = 64
GQA_HEADS = 16
GQA_KV = 4
MLA_HEADS = 4
MLA_NOPE = 128
MLA_V = 128
MLA_QR = 512
MLA_KVR = 256
ML_HEADS = 4
ML_D = 128
ML_W = ML_HEADS * ML_D
CHUNK = 128
N_EXPERTS = 8
RMS_EPS = 1e-6
LN_EPS = 1e-5
DEPTH = 2
ALPHA = (2 * DEPTH) ** 0.25

LANES = 128
NEG_BIG = -1e30
LOG2E = 1.4426950408889634
ONES_ROWS = 16
ATTN_BUFS = 4
ATTN_AHEAD = 1
VMEM_LIMIT = 56 * 1024 * 1024
FFN_TM = 1024
GATHER_ROWS_PER_STEP = 64


def _cp(*sem, vmem=VMEM_LIMIT):
    return pltpu.CompilerParams(dimension_semantics=sem, vmem_limit_bytes=vmem)


def _full(shape):
    n = len(shape)
    return pl.BlockSpec(shape, lambda *_: (0,) * n)


def _bf(x):
    return x.astype(BF16)


def _dot(a, b):
    return jnp.dot(a, b, preferred_element_type=F32)


def _split3(x):
    hi = _bf(x)
    r1 = x - hi.astype(F32)
    mid = _bf(r1)
    lo = _bf(r1 - mid.astype(F32))
    return hi, mid, lo


def _ada_kernel(c_ref, w_ref, b_ref, o_ref):
    c = c_ref[...]
    ca = c * jax.nn.sigmoid(c)
    o_ref[...] = jnp.sum(ca * w_ref[...], axis=0, keepdims=True) + b_ref[...]


def _ada(c_col, w, b):
    d, n = w.shape
    tn = 1024
    return pl.pallas_call(
        _ada_kernel,
        out_shape=jax.ShapeDtypeStruct((1, n), F32),
        grid=(n // tn,),
        in_specs=[_full((d, 1)), pl.BlockSpec((d, tn), lambda j: (0, j)), pl.BlockSpec((1, tn), lambda j: (0, j))],
        out_specs=pl.BlockSpec((1, tn), lambda j: (0, j)),
        compiler_params=_cp("parallel"),
        name="ada_mod",
    )(c_col, w, b.reshape(1, n))


def _group_sumsq(blk, bd):
    sq = blk * blk
    hi = _bf(sq)
    lo = _bf(sq - hi.astype(F32))
    return _dot(hi, bd) + _dot(lo, bd)


def _rope(blk, cos, sin_a, sin_b):
    return blk * cos + pltpu.roll(blk, LANES - 16, 1) * sin_a + pltpu.roll(blk, 16, 1) * sin_b


def _modulate(x, scale, shift):
    return x * (1.0 + scale) + shift


def _ones_rows(n):
    r = lax.broadcasted_iota(jnp.int32, (ONES_ROWS, n), 0)
    return jnp.where(r == 0, 1.0, 0.0).astype(BF16)


def _inproj_gqa_kernel(x_ref, sc_ref, sh_ref, w_ref, qg_ref, kg_ref, cos_ref, sa_ref, sb_ref, bd_ref,
                       qT_ref, k_ref, vT_ref):
    u = _bf(_modulate(x_ref[...], sc_ref[...], sh_ref[...]))
    y = _dot(u, w_ref[...])
    cos, sa, sb, bd = cos_ref[...], sa_ref[...], sb_ref[...], bd_ref[...]
    nq = GQA_HEADS * HEAD64 // LANES
    nk = GQA_KV * HEAD64 // LANES
    scale = HEAD64 ** -0.5 * LOG2E
    for j in range(nq):
        blk = y[:, j * LANES:(j + 1) * LANES]
        ss = _group_sumsq(blk, bd)
        blk = blk * lax.rsqrt(ss * (1.0 / HEAD64) + RMS_EPS) * qg_ref[...]
        blk = _rope(blk, cos, sa, sb) * scale
        qT_ref[j * LANES:(j + 1) * LANES, :] = _bf(blk.T)
    for j in range(nk):
        blk = y[:, (nq + j) * LANES:(nq + j + 1) * LANES]
        ss = _group_sumsq(blk, bd)
        blk = blk * lax.rsqrt(ss * (1.0 / HEAD64) + RMS_EPS) * kg_ref[...]
        blk = _bf(_rope(blk, cos, sa, sb))
        k_ref[2 * j] = blk[:, :HEAD64]
        k_ref[2 * j + 1] = blk[:, HEAD64:]
    vT = _bf(y[:, (nq + nk) * LANES:].T)
    ones = _ones_rows(vT.shape[1])
    dve = HEAD64 + ONES_ROWS
    for h in range(GQA_KV):
        vT_ref[h * dve:h * dve + HEAD64, :] = vT[h * HEAD64:(h + 1) * HEAD64, :]
        vT_ref[h * dve + HEAD64:(h + 1) * dve, :] = ones


def _inproj_gqa(x, sc, sh, w, qg, kg, cos, sa, sb, bd, tm=256):
    s, d = x.shape
    n = w.shape[1]
    row = lambda i: (i, 0)
    return pl.pallas_call(
        _inproj_gqa_kernel,
        out_shape=(jax.ShapeDtypeStruct((GQA_HEADS * HEAD64, s), BF16),
                   jax.ShapeDtypeStruct((GQA_KV, s, HEAD64), BF16),
                   jax.ShapeDtypeStruct((GQA_KV * (HEAD64 + ONES_ROWS), s), BF16)),
        grid=(s // tm,),
        in_specs=[pl.BlockSpec((tm, d), row), _full((1, d)), _full((1, d)), _full((d, n)),
                  _full((1, LANES)), _full((1, LANES)),
                  pl.BlockSpec((tm, LANES), row), pl.BlockSpec((tm, LANES), row), pl.BlockSpec((tm, LANES), row),
                  _full((LANES, LANES))],
        out_specs=(pl.BlockSpec((GQA_HEADS * HEAD64, tm), lambda i: (0, i)),
                   pl.BlockSpec((GQA_KV, tm, HEAD64), lambda i: (0, i, 0)),
                   pl.BlockSpec((GQA_KV * (HEAD64 + ONES_ROWS), tm), lambda i: (0, i))),
        compiler_params=_cp("parallel"),
        name="inproj_gqa",
    )(x, sc, sh, w, qg, kg, cos, sa, sb, bd)


def _inproj_mla_kernel(x_ref, sc_ref, sh_ref, w_ref, qn_ref, kvn_ref, wuq_ref, wukv_ref, cos_ref, sa_ref, sb_ref,
                       qT_ref, k_ref, vT_ref):
    u = _bf(_modulate(x_ref[...], sc_ref[...], sh_ref[...]))
    y = _dot(u, w_ref[...])
    cos, sa, sb = cos_ref[...], sa_ref[...], sb_ref[...]
    cq = y[:, :MLA_QR]
    ckv = y[:, MLA_QR:MLA_QR + MLA_KVR]
    kr = y[:, MLA_QR + MLA_KVR:]
    cqn = cq * lax.rsqrt(jnp.mean(cq * cq, axis=-1, keepdims=True) + RMS_EPS) * qn_ref[...]
    ckvn = ckv * lax.rsqrt(jnp.mean(ckv * ckv, axis=-1, keepdims=True) + RMS_EPS) * kvn_ref[...]
    qm = _dot(_bf(cqn), wuq_ref[...])
    kvm = _dot(_bf(ckvn), wukv_ref[...])
    scale = (MLA_NOPE + HEAD64) ** -0.5 * LOG2E
    dq = MLA_NOPE + HEAD64
    dve = MLA_V + ONES_ROWS
    ones = _ones_rows(y.shape[0])
    k_rope = _bf(_rope(kr, cos, sa, sb)[:, :HEAD64])
    for p in range(MLA_HEADS // 2):
        rblk = qm[:, MLA_HEADS * MLA_NOPE + p * LANES:MLA_HEADS * MLA_NOPE + (p + 1) * LANES]
        rT = _bf((_rope(rblk, cos, sa, sb) * scale).T)
        for e in range(2):
            h = 2 * p + e
            qT_ref[h * dq + MLA_NOPE:(h + 1) * dq, :] = rT[e * HEAD64:(e + 1) * HEAD64, :]
    for h in range(MLA_HEADS):
        nope = qm[:, h * MLA_NOPE:(h + 1) * MLA_NOPE] * scale
        qT_ref[h * dq:h * dq + MLA_NOPE, :] = _bf(nope.T)
        k_ref[h, :, :MLA_NOPE] = _bf(kvm[:, h * 2 * LANES:h * 2 * LANES + MLA_NOPE])
        k_ref[h, :, MLA_NOPE:] = k_rope
        vT_ref[h * dve:h * dve + MLA_V, :] = _bf(kvm[:, h * 2 * LANES + MLA_NOPE:(h + 1) * 2 * LANES].T)
        vT_ref[h * dve + MLA_V:(h + 1) * dve, :] = ones


def _inproj_mla(x, sc, sh, w, qn, kvn, wuq, wukv, cos, sa, sb, tm=256):
    s, d = x.shape
    n = w.shape[1]
    dq = MLA_NOPE + HEAD64
    row = lambda i: (i, 0)
    return pl.pallas_call(
        _inproj_mla_kernel,
        out_shape=(jax.ShapeDtypeStruct((MLA_HEADS * dq, s), BF16),
                   jax.ShapeDtypeStruct((MLA_HEADS, s, dq), BF16),
                   jax.ShapeDtypeStruct((MLA_HEADS * (MLA_V + ONES_ROWS), s), BF16)),
        grid=(s // tm,),
        in_specs=[pl.BlockSpec((tm, d), row), _full((1, d)), _full((1, d)), _full((d, n)),
                  _full((1, MLA_QR)), _full((1, MLA_KVR)), _full(wuq.shape), _full(wukv.shape),
                  pl.BlockSpec((tm, LANES), row), pl.BlockSpec((tm, LANES), row), pl.BlockSpec((tm, LANES), row)],
        out_specs=(pl.BlockSpec((MLA_HEADS * dq, tm), lambda i: (0, i)),
                   pl.BlockSpec((MLA_HEADS, tm, dq), lambda i: (0, i, 0)),
                   pl.BlockSpec((MLA_HEADS * (MLA_V + ONES_ROWS), tm), lambda i: (0, i))),
        compiler_params=_cp("parallel"),
        name="inproj_mla",
    )(x, sc, sh, w, qn, kvn, wuq, wukv, cos, sa, sb)


def _inproj_mlstm_kernel(x_ref, xp_ref, xn_ref, sc_ref, sh_ref, w_ref, conv_ref, gb_ref,
                         q_ref, kT_ref, v_ref, o_ref, gT_ref):
    i = pl.program_id(0)
    last = pl.num_programs(0) - 1
    sc, sh = sc_ref[...], sh_ref[...]
    u = _bf(_modulate(x_ref[...], sc, sh))
    y = _dot(u, w_ref[...])
    tm = y.shape[0]
    qk = y[:, :2 * ML_W]
    halo = jnp.concatenate([xp_ref[...], xn_ref[...]], axis=0)
    hy = _dot(_bf(_modulate(halo, sc, sh)), w_ref[:, :2 * ML_W])
    prev_row = jnp.where(i == 0, 0.0, hy[7:8, :])
    next_row = jnp.where(i == last, 0.0, hy[8:9, :])
    rows = lax.broadcasted_iota(jnp.int32, qk.shape, 0)
    prev = jnp.where(rows == 0, prev_row, pltpu.roll(qk, 1, 0))
    nxt = jnp.where(rows == tm - 1, next_row, pltpu.roll(qk, tm - 1, 0))
    cw = conv_ref[...]
    z = cw[0:1, :] * prev + cw[1:2, :] * qk + cw[2:3, :] * nxt
    z = z * jax.nn.sigmoid(z)
    q_ref[...] = _bf(z[:, :ML_W])
    kT_ref[...] = (z[:, ML_W:] * (ML_D ** -0.5)).T
    v_ref[...] = _bf(y[:, 2 * ML_W:3 * ML_W])
    o_ref[...] = y[:, 3 * ML_W:4 * ML_W]
    g = y[:, 4 * ML_W:] + gb_ref[...]
    gT_ref[...] = g.T[:4 * ML_HEADS, :]


def _inproj_mlstm(x, sc, sh, w, conv, gb, tm=256):
    s, d = x.shape
    n = w.shape[1]
    row = lambda i: (i, 0)
    col = lambda i: (0, i)
    r8 = tm // 8
    return pl.pallas_call(
        _inproj_mlstm_kernel,
        out_shape=(jax.ShapeDtypeStruct((s, ML_W), BF16), jax.ShapeDtypeStruct((ML_W, s), F32),
                   jax.ShapeDtypeStruct((s, ML_W), BF16), jax.ShapeDtypeStruct((s, ML_W), F32),
                   jax.ShapeDtypeStruct((4 * ML_HEADS, s), F32)),
        grid=(s // tm,),
        in_specs=[pl.BlockSpec((tm, d), row),
                  pl.BlockSpec((8, d), lambda i: (jnp.maximum(i * r8 - 1, 0), 0)),
                  pl.BlockSpec((8, d), lambda i: (jnp.minimum((i + 1) * r8, s // 8 - 1), 0)),
                  _full((1, d)), _full((1, d)), _full((d, n)), _full((3, 2 * ML_W)), _full((1, LANES))],
        out_specs=(pl.BlockSpec((tm, ML_W), row), pl.BlockSpec((ML_W, tm), col),
                   pl.BlockSpec((tm, ML_W), row), pl.BlockSpec((tm, ML_W), row),
                   pl.BlockSpec((4 * ML_HEADS, tm), col)),
        compiler_params=_cp("parallel"),
        name="inproj_mlstm",
    )(x, x, x, sc, sh, w, conv, gb)


def _attn_kernel(qT_ref, k_ref, vT_ref, o_ref, s_sc, acc_sc, *, groups, dq, dv, tk):
    hp = k_ref.shape[0]
    seq = k_ref.shape[1]
    n = seq // tk
    dve = dv + ONES_ROWS
    cq = qT_ref.shape[1]
    nq = groups * cq
    qTs = []
    for hh in range(hp):
        r0 = hh * groups * dq
        qTs.append(jnp.concatenate([qT_ref[r0 + g * dq:r0 + (g + 1) * dq, :] for g in range(groups)], axis=1)
                   if groups > 1 else qT_ref[r0:r0 + dq, :])
    acc_sc[...] = jnp.zeros(acc_sc.shape, F32)

    def scores(hh, c, buf):
        off = c * tk if isinstance(c, int) else pl.multiple_of(c * tk, tk)
        s = _dot(k_ref[hh, pl.ds(off, tk), :], qTs[hh])
        s_sc[hh, buf] = s
        return jnp.max(s, axis=0, keepdims=True)

    def accumulate(hh, c, buf, mx, m_old):
        off = c * tk if isinstance(c, int) else pl.multiple_of(c * tk, tk)
        m_new = jnp.maximum(m_old, mx)
        p = jnp.exp2(_bf(s_sc[hh, buf] - m_new))
        pv = _dot(vT_ref[hh * dve:(hh + 1) * dve, pl.ds(off, tk)], p)
        acc_sc[hh] = jnp.exp2(m_old - m_new) * acc_sc[hh] + pv
        return m_new

    def body(base, carry, tail=False):
        mxs, ms = [list(t) for t in carry[0]], list(carry[1])
        for j in range(ATTN_BUFS):
            for hh in range(hp):
                if not (tail and j + ATTN_AHEAD >= ATTN_BUFS):
                    mxs[hh].append(scores(hh, base + j + ATTN_AHEAD, (j + ATTN_AHEAD) % ATTN_BUFS))
                ms[hh] = accumulate(hh, base + j, j, mxs[hh].pop(0), ms[hh])
        return tuple(tuple(t) for t in mxs), tuple(ms)

    carry = (tuple(tuple(scores(hh, c, c) for c in range(ATTN_AHEAD)) for hh in range(hp)),
             tuple(jnp.full((1, nq), NEG_BIG, F32) for _ in range(hp)))
    carry = lax.fori_loop(0, n // ATTN_BUFS - 1, lambda i, cr: body(i * ATTN_BUFS, cr), carry)
    body(n - ATTN_BUFS, carry, tail=True)
    for hh in range(hp):
        oT = acc_sc[hh, :dv, :] * (1.0 / acc_sc[hh, dv:dv + 1, :])
        if groups > 1:
            oT = jnp.concatenate([oT[:, g * cq:(g + 1) * cq] for g in range(groups)], axis=0)
        o_ref[:, hh * groups * dv:(hh + 1) * groups * dv] = _bf(oT.T)


def _attention(qT, k, vT, *, groups, dq, dv, cq, hp, tk=512):
    h, s, _ = k.shape
    nq = groups * cq
    dve = dv + ONES_ROWS
    assert (s // tk) % ATTN_BUFS == 0 and h % hp == 0
    kern = functools.partial(_attn_kernel, groups=groups, dq=dq, dv=dv, tk=tk)
    return pl.pallas_call(
        kern,
        out_shape=jax.ShapeDtypeStruct((s, h * groups * dv), BF16),
        grid=(h // hp, s // cq),
        in_specs=[pl.BlockSpec((hp * groups * dq, cq), lambda a, i: (a, i)),
                  pl.BlockSpec((hp, s, dq), lambda a, i: (a, 0, 0)),
                  pl.BlockSpec((hp * dve, s), lambda a, i: (a, 0))],
        out_specs=pl.BlockSpec((cq, hp * groups * dv), lambda a, i: (i, a)),
        scratch_shapes=[pltpu.VMEM((hp, ATTN_BUFS, tk, nq), F32), pltpu.VMEM((hp, dve, nq), F32)],
        compiler_params=_cp("parallel", "parallel"),
        name="attention_g%d" % groups,
    )(qT, k, vT)


def _mlstm_kernel(q_ref, kT_ref, v_ref, o_ref, g_ref, hn_ref, out_ref,
                  cn_sc, mprev_sc, w_sc, r_sc, lf_sc, g_sc, mloc_sc, hacc_sc):
    head = pl.program_id(0)
    nc = g_ref.shape[1]
    L = CHUNK
    ii = lax.broadcasted_iota(jnp.int32, (L, L), 0)
    jj = lax.broadcasted_iota(jnp.int32, (L, L), 1)
    e0 = jnp.where(lax.broadcasted_iota(jnp.int32, (L, LANES), 1) == 0, 1.0, 0.0).astype(BF16)

    for direction in range(2):
        fw = direction == 0
        mask = (jj <= ii) if fw else (jj >= ii)
        tri = jnp.where((ii <= jj) if fw else (ii >= jj), 1.0, 0.0).astype(BF16)
        li = g_ref[2 * ML_HEADS * direction + head]
        fg = g_ref[2 * ML_HEADS * direction + ML_HEADS + head]
        lf = jnp.minimum(fg, 0.0) - jnp.log1p(jnp.exp(-jnp.abs(fg)))
        hi, mid, lo = _split3(lf)
        b = _dot(hi, tri) + _dot(mid, tri) + _dot(lo, tri)
        gtot = jnp.sum(lf, axis=-1, keepdims=True)
        a = gtot - b + li
        mloc = jnp.max(a, axis=-1, keepdims=True)
        w_sc[...] = jnp.exp(a - mloc)
        r_sc[...] = li - b
        lf_sc[...] = lf
        g_sc[...] = jnp.broadcast_to(gtot, (nc, LANES))
        mloc_sc[...] = jnp.broadcast_to(mloc, (nc, LANES))

        def scan_body(i, carry, fw=fw):
            cn, m = carry
            cc = i if fw else nc - 1 - i
            off = pl.multiple_of(cc * L, L)
            cn_sc[cc] = _bf(cn)
            mprev_sc[pl.ds(cc, 1), :] = jnp.broadcast_to(m, (1, LANES))
            kw = _bf(kT_ref[:, pl.ds(off, L)] * w_sc[pl.ds(cc, 1), :])
            vext = jnp.concatenate([v_ref[pl.ds(off, L), :], e0], axis=1)
            kv = _dot(kw, vext)
            g = g_sc[pl.ds(cc, 1), 0:1]
            ml = mloc_sc[pl.ds(cc, 1), 0:1]
            m_new = jnp.maximum(g + m, ml)
            cn = jnp.exp(g + m - m_new) * cn + jnp.exp(ml - m_new) * kv
            return cn, m_new

        lax.fori_loop(0, nc, scan_body, (jnp.zeros((ML_D, 2 * ML_D), F32), jnp.zeros((1, 1), F32)), unroll=2)

        def out_body(cc, carry, fw=fw, mask=mask):
            off = pl.multiple_of(cc * L, L)
            mp = mprev_sc[pl.ds(cc, 1), 0:1]
            rm = jnp.where(mask, r_sc[pl.ds(cc, 1), :], -jnp.inf)
            mt = jnp.maximum(mp, jnp.max(rm, axis=-1, keepdims=True))
            d = jnp.exp(rm - mt)
            iw = jnp.exp(mp - mt)
            qc = q_ref[pl.ds(off, L), :]
            qk = _dot(qc, _bf(kT_ref[:, pl.ds(off, L)])) * d
            num_i = _dot(_bf(qk), v_ref[pl.ds(off, L), :])
            den_i = jnp.sum(qk, axis=-1, keepdims=True)
            inter = _dot(qc, cn_sc[cc])
            num = num_i + iw * inter[:, :ML_D]
            den = den_i + iw * inter[:, ML_D:ML_D + 1]
            bcol = jnp.sum(jnp.where(mask, lf_sc[pl.ds(cc, 1), :], 0.0), axis=-1, keepdims=True)
            h = num / jnp.maximum(jnp.abs(den), jnp.exp(-(bcol + mt)))
            if fw:
                hacc_sc[pl.ds(off, L), :] = h
            else:
                h = h + hacc_sc[pl.ds(off, L), :]
                mu = jnp.mean(h, axis=-1, keepdims=True)
                hc = h - mu
                var = jnp.mean(hc * hc, axis=-1, keepdims=True)
                hn = hc * lax.rsqrt(var + LN_EPS) * hn_ref[...]
                out_ref[pl.ds(off, L), :] = _bf(jax.nn.sigmoid(o_ref[pl.ds(off, L), :]) * hn)
            return carry

        lax.fori_loop(0, nc, out_body, 0, unroll=4)


def _mlstm(q, kT, v, o, g3, hn):
    s = q.shape[0]
    nc = s // CHUNK
    colblk = lambda h: (0, h)
    return pl.pallas_call(
        _mlstm_kernel,
        out_shape=jax.ShapeDtypeStruct((s, ML_W), BF16),
        grid=(ML_HEADS,),
        in_specs=[pl.BlockSpec((s, ML_D), colblk), pl.BlockSpec((ML_D, s), lambda h: (h, 0)),
                  pl.BlockSpec((s, ML_D), colblk), pl.BlockSpec((s, ML_D), colblk),
                  _full(g3.shape), pl.BlockSpec((1, ML_D), colblk)],
        out_specs=pl.BlockSpec((s, ML_D), colblk),
        scratch_shapes=[pltpu.VMEM((nc, ML_D, 2 * ML_D), BF16), pltpu.VMEM((nc, LANES), F32),
                        pltpu.VMEM((nc, CHUNK), F32), pltpu.VMEM((nc, CHUNK), F32), pltpu.VMEM((nc, CHUNK), F32),
                        pltpu.VMEM((nc, LANES), F32), pltpu.VMEM((nc, LANES), F32),
                        pltpu.VMEM((s, ML_D), F32)],
        compiler_params=_cp("parallel"),
        name="mlstm",
    )(q, kT, v, o, g3, hn)


def _layer_norm(z, g, b):
    mu = jnp.mean(z, axis=-1, keepdims=True)
    zc = z - mu
    var = jnp.mean(zc * zc, axis=-1, keepdims=True)
    return zc * lax.rsqrt(var + LN_EPS) * g + b


def _outproj_kernel(*refs, route):
    if route:
        (x_ref, a_ref, b_ref, c_ref, w_ref, gate_ref, g_ref, bb_ref, sc_ref, sh_ref, rw_both_ref, rw_high_ref,
         x1_ref, u2_ref, lg_ref) = refs
    else:
        x_ref, a_ref, b_ref, c_ref, w_ref, gate_ref, g_ref, bb_ref, sc_ref, sh_ref, x1_ref, u2_ref = refs
    na, nb = a_ref.shape[1], b_ref.shape[1]
    y = _dot(a_ref[...], w_ref[:na, :]) + _dot(b_ref[...], w_ref[na:na + nb, :]) + _dot(c_ref[...], w_ref[na + nb:, :])
    x1 = _layer_norm(ALPHA * x_ref[...] + gate_ref[...] * y, g_ref[...], bb_ref[...])
    x1_ref[...] = x1
    u2 = _modulate(x1, sc_ref[...], sh_ref[...])
    u2_ref[...] = u2.astype(u2_ref.dtype)
    if route:
        hi = _bf(u2)
        lo = _bf(u2 - hi.astype(F32))
        both = _dot(hi, rw_both_ref[...])
        lg_ref[...] = both[:, :LANES] + (_dot(lo, rw_high_ref[...]) + both[:, LANES:])


def _outproj(x, a, b, c, w, gate, g, bb, sc, sh, rw=None, tm=512):
    s, d = x.shape
    row = lambda i: (i, 0)
    route = rw is not None
    ins = [x, a, b, c, w, gate, g, bb, sc, sh]
    in_specs = [pl.BlockSpec((tm, d), row), pl.BlockSpec((tm, a.shape[1]), row), pl.BlockSpec((tm, b.shape[1]), row),
                pl.BlockSpec((tm, c.shape[1]), row), _full(w.shape)] + [_full((1, d))] * 5
    out_shape = [jax.ShapeDtypeStruct((s, d), F32), jax.ShapeDtypeStruct((s, d), F32 if route else BF16)]
    out_specs = [pl.BlockSpec((tm, d), row), pl.BlockSpec((tm, d), row)]
    if route:
        rwp = jnp.zeros((d, LANES), F32).at[:, :rw.shape[1]].set(rw)
        rwh = _bf(rwp)
        rwl = _bf(rwp - rwh.astype(F32))
        ins += [jnp.concatenate([rwh, rwl], axis=1), rwh]
        in_specs += [_full((d, 2 * LANES)), _full((d, LANES))]
        out_shape.append(jax.ShapeDtypeStruct((s, LANES), F32))
        out_specs.append(pl.BlockSpec((tm, LANES), row))
    return pl.pallas_call(
        functools.partial(_outproj_kernel, route=route),
        out_shape=tuple(out_shape),
        grid=(s // tm,),
        in_specs=in_specs,
        out_specs=tuple(out_specs),
        compiler_params=_cp("parallel"),
        name="outproj_ln1",
    )(*ins)


def _swiglu_partial(x, wg, wu, wd):
    gt = _dot(x, _bf(wg))
    up = _dot(x, _bf(wu))
    return _dot(_bf(gt * jax.nn.sigmoid(gt) * up), _bf(wd))


def _ffn_dense_kernel(x_ref, wg_ref, wu_ref, wd_ref, y_ref):
    j = pl.program_id(1)

    @pl.when(j == 0)
    def _():
        y_ref[...] = _swiglu_partial(x_ref[...], wg_ref[...], wu_ref[...], wd_ref[...])

    @pl.when(j > 0)
    def _():
        y_ref[...] += _swiglu_partial(x_ref[...], wg_ref[...], wu_ref[...], wd_ref[...])


def _ffn_dense(x, wg, wu, wd, tm=FFN_TM, tf=256):
    s, d = x.shape
    f = wg.shape[1]
    return pl.pallas_call(
        _ffn_dense_kernel,
        out_shape=jax.ShapeDtypeStruct((s, d), F32),
        grid=(s // tm, f // tf),
        in_specs=[pl.BlockSpec((tm, d), lambda i, j: (i, 0)), pl.BlockSpec((d, tf), lambda i, j: (0, j)),
                  pl.BlockSpec((d, tf), lambda i, j: (0, j)), pl.BlockSpec((tf, d), lambda i, j: (j, 0))],
        out_specs=pl.BlockSpec((tm, d), lambda i, j: (i, 0)),
        compiler_params=_cp("parallel", "arbitrary"),
        name="swiglu_dense",
    )(x, wg, wu, wd)


def _ffn_routed_kernel(te_ref, nu_ref, src_ref, u_hbm, wg_ref, wu_ref, wd_ref, y_ref, xf_buf, xb_buf, sem):
    i, j = pl.program_id(0), pl.program_id(1)
    n_live = nu_ref[0]
    live = i < n_live
    tm = y_ref.shape[0]

    def row_copy(tile, r):
        return pltpu.make_async_copy(u_hbm.at[pl.ds(src_ref[tile * tm + r], 1)], xf_buf.at[pl.ds(r, 1)], sem)

    def issue(tile, first, count):
        def body(r, carry):
            row_copy(tile, first + r).start()
            return carry
        lax.fori_loop(0, count, body, 0, unroll=8)

    @pl.when((i == 0) & (j == 0))
    def _():
        issue(0, 0, tm)

    @pl.when(live & (j == 0))
    def _():
        pltpu.make_async_copy(u_hbm.at[pl.ds(0, tm)], xf_buf, sem).wait()
        xb_buf[...] = _bf(xf_buf[...])
        y_ref[...] = _swiglu_partial(xb_buf[...], wg_ref[0], wu_ref[0], wd_ref[0])

    @pl.when((j >= 1) & (j <= tm // GATHER_ROWS_PER_STEP) & (i + 1 < n_live))
    def _():
        issue(i + 1, (j - 1) * GATHER_ROWS_PER_STEP, GATHER_ROWS_PER_STEP)

    @pl.when(live & (j > 0))
    def _():
        y_ref[...] += _swiglu_partial(xb_buf[...], wg_ref[0], wu_ref[0], wd_ref[0])

    @pl.when(jnp.logical_not(live) & (j == 0))
    def _():
        y_ref[...] = jnp.zeros(y_ref.shape, F32)


def _ffn_routed(u, src, wg, wu, wd, tile_expert, n_used, tm, tf=256):
    s, d = u.shape
    f = wg.shape[2]
    nt, nf = src.shape[0] // tm, f // tf
    assert tm % GATHER_ROWS_PER_STEP == 0 and nf > tm // GATHER_ROWS_PER_STEP

    def clamp(i, j, nu):
        live = i < nu[0]
        return jnp.where(live, i, nu[0] - 1), jnp.where(live, j, nf - 1)

    def gmap(i, j, te, nu, sr):
        ii, jj = clamp(i, j, nu)
        return te[ii], 0, jj

    def dmap(i, j, te, nu, sr):
        ii, jj = clamp(i, j, nu)
        return te[ii], jj, 0

    return pl.pallas_call(
        _ffn_routed_kernel,
        out_shape=jax.ShapeDtypeStruct((nt * tm, d), F32),
        grid_spec=pltpu.PrefetchScalarGridSpec(
            num_scalar_prefetch=3,
            grid=(nt, nf),
            in_specs=[pl.BlockSpec(memory_space=pl.ANY), pl.BlockSpec((1, d, tf), gmap), pl.BlockSpec((1, d, tf), gmap),
                      pl.BlockSpec((1, tf, d), dmap)],
            out_specs=pl.BlockSpec((tm, d), lambda i, j, te, nu, sr: (i, 0)),
            scratch_shapes=[pltpu.VMEM((tm, d), F32), pltpu.VMEM((tm, d), BF16), pltpu.SemaphoreType.DMA(())]),
        compiler_params=_cp("arbitrary", "arbitrary"),
        name="swiglu_routed",
    )(tile_expert, n_used, src, u, wg, wu, wd)


def _ln2_kernel(x_ref, f_ref, gate_ref, g_ref, b_ref, o_ref):
    o_ref[...] = _layer_norm(ALPHA * x_ref[...] + gate_ref[...] * f_ref[...], g_ref[...], b_ref[...])


def _ln2(x, f, gate, g, b, tm=512):
    s, d = x.shape
    row = lambda i: (i, 0)
    return pl.pallas_call(
        _ln2_kernel,
        out_shape=jax.ShapeDtypeStruct((s, d), F32),
        grid=(s // tm,),
        in_specs=[pl.BlockSpec((tm, d), row), pl.BlockSpec((tm, d), row)] + [_full((1, d))] * 3,
        out_specs=pl.BlockSpec((tm, d), row),
        compiler_params=_cp("parallel"),
        name="residual_ln2",
    )(x, f, gate, g, b)


def _row_copy(y_hbm, idx_ref, buf, sem, tile, slot, r, tm):
    return pltpu.make_async_copy(y_hbm.at[pl.ds(idx_ref[tile * tm + r], 1)], buf.at[slot, pl.ds(r, 1)], sem.at[slot])


def _ln2_combine_kernel(d1_ref, d2_ref, x_ref, info_ref, y_hbm, gate_ref, g_ref, b_ref, o_ref, a_buf, b_buf, sem):
    i = pl.program_id(0)
    n = pl.num_programs(0)
    tm = x_ref.shape[0]

    def issue(tile, slot):
        def body(r, carry):
            _row_copy(y_hbm, d1_ref, a_buf, sem, tile, slot, r, tm).start()
            _row_copy(y_hbm, d2_ref, b_buf, sem, tile, slot, r, tm).start()
            return carry
        lax.fori_loop(0, tm, body, 0, unroll=8)

    @pl.when(i == 0)
    def _():
        issue(0, 0)

    @pl.when(i + 1 < n)
    def _():
        issue(i + 1, (i + 1) % 2)

    slot = i % 2
    pltpu.make_async_copy(y_hbm.at[pl.ds(0, tm)], a_buf.at[slot], sem.at[slot]).wait()
    pltpu.make_async_copy(y_hbm.at[pl.ds(0, tm)], b_buf.at[slot], sem.at[slot]).wait()
    info = info_ref[...]
    f = info[:, 2:3] * a_buf[slot] + info[:, 3:4] * b_buf[slot]
    o_ref[...] = _layer_norm(ALPHA * x_ref[...] + gate_ref[...] * f, g_ref[...], b_ref[...])


def _ln2_combine(x, info, y, d1, d2, gate, g, b, tm=256):
    s, d = x.shape
    row = lambda i, *_: (i, 0)
    one = lambda i, *_: (0, 0)
    return pl.pallas_call(
        _ln2_combine_kernel,
        out_shape=jax.ShapeDtypeStruct((s, d), F32),
        grid_spec=pltpu.PrefetchScalarGridSpec(
            num_scalar_prefetch=2,
            grid=(s // tm,),
            in_specs=[pl.BlockSpec((tm, d), row), pl.BlockSpec((tm, LANES), row), pl.BlockSpec(memory_space=pl.ANY),
                      pl.BlockSpec((1, d), one), pl.BlockSpec((1, d), one), pl.BlockSpec((1, d), one)],
            out_specs=pl.BlockSpec((tm, d), row),
            scratch_shapes=[pltpu.VMEM((2, tm, d), F32), pltpu.VMEM((2, tm, d), F32), pltpu.SemaphoreType.DMA((2,))]),
        compiler_params=_cp("arbitrary"),
        name="combine_ln2",
    )(d1, d2, x, info, y, gate, g, b)


def _router_kernel(lg_ref, info_ref, cnt_ref, carry_sc):
    i = pl.program_id(0)

    @pl.when(i == 0)
    def _():
        carry_sc[...] = jnp.zeros(carry_sc.shape, F32)

    lg = lg_ref[...]
    tm = lg.shape[0]
    lane = lax.broadcasted_iota(jnp.int32, lg.shape, 1)
    lg = jnp.where(lane < N_EXPERTS, lg, -jnp.inf)
    v1 = jnp.max(lg, axis=-1, keepdims=True)
    i1 = jnp.min(jnp.where(lg == v1, lane, LANES), axis=-1, keepdims=True)
    lg2 = jnp.where(lane == i1, -jnp.inf, lg)
    v2 = jnp.max(lg2, axis=-1, keepdims=True)
    i2 = jnp.min(jnp.where(lg2 == v2, lane, LANES), axis=-1, keepdims=True)
    e21 = jnp.exp(v2 - v1)
    w1 = 1.0 / (1.0 + e21)
    w2 = e21 * w1
    oh1 = lane == i1
    oh2 = lane == i2
    oh = jnp.where(oh1 | oh2, 1.0, 0.0)
    rr = lax.broadcasted_iota(jnp.int32, (tm, tm), 0)
    cc = lax.broadcasted_iota(jnp.int32, (tm, tm), 1)
    strict = jnp.where(cc < rr, 1.0, 0.0).astype(BF16)
    before = _dot(strict, _bf(oh)) + carry_sc[...]
    r1 = jnp.sum(jnp.where(oh1, before, 0.0), axis=-1, keepdims=True)
    r2 = jnp.sum(jnp.where(oh2, before, 0.0), axis=-1, keepdims=True)
    vals = (i1.astype(F32), i2.astype(F32), w1, w2, r1, r2)
    info = jnp.zeros(lg.shape, F32)
    for n, v in enumerate(vals):
        info = jnp.where(lane == n, v, info)
    info_ref[...] = info
    carry_sc[...] += jnp.sum(oh, axis=0, keepdims=True)
    cnt_ref[...] = carry_sc[...]


def _router(logits, tm=512):
    s = logits.shape[0]
    return pl.pallas_call(
        _router_kernel,
        out_shape=(jax.ShapeDtypeStruct((s, LANES), F32), jax.ShapeDtypeStruct((1, LANES), F32)),
        grid=(s // tm,),
        in_specs=[pl.BlockSpec((tm, LANES), lambda i: (i, 0))],
        out_specs=(pl.BlockSpec((tm, LANES), lambda i: (i, 0)), _full((1, LANES))),
        scratch_shapes=[pltpu.VMEM((1, LANES), F32)],
        compiler_params=_cp("arbitrary"),
        name="router_top2",
    )(logits)


def _rope_tables(s):
    rows = s // GRID_W
    n_freq = HEAD64 // 4
    inv_freq = ROPE_THETA ** (-jnp.arange(n_freq, dtype=F32) / n_freq)
    ang_r = jnp.arange(rows, dtype=F32)[:, None] * inv_freq
    ang_c = jnp.arange(GRID_W, dtype=F32)[:, None] * inv_freq

    def table(fn):
        tr = jnp.broadcast_to(fn(ang_r)[:, None, :], (rows, GRID_W, n_freq)).reshape(s, n_freq)
        tc = jnp.broadcast_to(fn(ang_c)[None, :, :], (rows, GRID_W, n_freq)).reshape(s, n_freq)
        return jnp.concatenate([tr, tr, tc, tc] * 2, axis=-1)

    cos, sin = table(jnp.cos), table(jnp.sin)
    first_half = (jnp.arange(LANES) % 32) < 16
    return cos, jnp.where(first_half, -sin, 0.0), jnp.where(first_half, 0.0, sin)


def _moe_ffn(u2, logits, wg, wu, wd, tm):
    s, d = u2.shape
    e = wg.shape[0]
    info, counts = _router(logits)
    e1, e2 = info[:, 0].astype(jnp.int32), info[:, 1].astype(jnp.int32)
    w1, w2 = info[:, 2], info[:, 3]
    r1, r2 = info[:, 4].astype(jnp.int32), info[:, 5].astype(jnp.int32)
    cnt = counts[0, :e].astype(jnp.int32)
    tiles = (cnt + tm - 1) // tm
    tile_end = jnp.cumsum(tiles)
    start = (tile_end - tiles) * tm
    nt = (2 * s) // tm + e
    n_used = tile_end[-1:].astype(jnp.int32)
    tile_expert = jnp.sum((jnp.arange(nt)[:, None] >= tile_end[None, :]).astype(jnp.int32), axis=1)
    tile_expert = jnp.minimum(tile_expert, e - 1)
    d1, d2 = start[e1] + r1, start[e2] + r2
    tok = jnp.arange(s, dtype=jnp.int32)
    src = jnp.zeros((nt * tm,), jnp.int32).at[d1].set(tok).at[d2].set(tok)
    y = _ffn_routed(u2, src, wg, wu, wd, tile_expert, n_used, tm=tm)
    return info, y, d1, d2


def _layer(x, c_col, tabs, p, moe):
    s, d = x.shape
    cos, sa, sb = tabs
    mod = _ada(c_col, p["ada_w"], p["ada_b"])
    sh1, sc1, g1, sh2, sc2, g2 = [mod[:, i * d:(i + 1) * d] for i in range(6)]

    w_in = p["w_in"]
    n_gqa = (GQA_HEADS + 2 * GQA_KV) * HEAD64
    n_mla = MLA_QR + MLA_KVR + HEAD64
    w_gqa = _bf(w_in[:, :n_gqa])
    w_mla = _bf(jnp.pad(w_in[:, n_gqa:n_gqa + n_mla], ((0, 0), (0, LANES - HEAD64))))
    w_ml = _bf(jnp.pad(w_in[:, n_gqa + n_mla:], ((0, 0), (0, LANES - 4 * ML_HEADS))))

    ii = jnp.arange(LANES)
    bd = _bf((ii[:, None] // HEAD64 == ii[None, :] // HEAD64).astype(F32))
    qg = jnp.tile(p["gqa_q_norm"], 2).reshape(1, LANES)
    kg = jnp.tile(p["gqa_k_norm"], 2).reshape(1, LANES)
    qT, k, vT = _inproj_gqa(x, sc1, sh1, w_gqa, qg, kg, cos, sa, sb, bd)
    out_a = _attention(qT, k, vT, groups=GQA_HEADS // GQA_KV, dq=HEAD64, dv=HEAD64, cq=128, hp=4)

    dq = MLA_NOPE + HEAD64
    wuq = p["mla_w_uq"].reshape(MLA_QR, MLA_HEADS, dq)
    wuq = _bf(jnp.concatenate([wuq[:, :, :MLA_NOPE].reshape(MLA_QR, -1), wuq[:, :, MLA_NOPE:].reshape(MLA_QR, -1)], axis=1))
    qTm, km, vTm = _inproj_mla(x, sc1, sh1, w_mla, p["mla_q_norm"].reshape(1, -1), p["mla_kv_norm"].reshape(1, -1),
                               wuq, _bf(p["mla_w_ukv"]), cos, sa, sb)
    out_b = _attention(qTm, km, vTm, groups=1, dq=dq, dv=MLA_V, cq=512, hp=2)

    gb = jnp.pad(p["mlstm_gate_b"], (0, LANES - 4 * ML_HEADS)).reshape(1, LANES)
    mq, mkT, mv, mo, gT = _inproj_mlstm(x, sc1, sh1, w_ml, p["mlstm_conv"], gb)
    out_c = _mlstm(mq, mkT, mv, mo, gT.reshape(4 * ML_HEADS, s // CHUNK, CHUNK), p["mlstm_head_norm"].reshape(1, -1))

    ln = [p[n].reshape(1, d) for n in ("ln1_g", "ln1_b", "ln2_g", "ln2_b")]
    if moe:
        x1, u2, logits = _outproj(x, out_a, out_b, out_c, _bf(p["w_out"]), g1, ln[0], ln[1], sc2, sh2, rw=p["router_w"])
        info, y, d1, d2 = _moe_ffn(u2, logits, p["moe_w_gate"], p["moe_w_up"], p["moe_w_down"], tm=FFN_TM)
        return _ln2_combine(x1, info, y, d1, d2, g2, ln[2], ln[3])
    else:
        x1, u2 = _outproj(x, out_a, out_b, out_c, _bf(p["w_out"]), g1, ln[0], ln[1], sc2, sh2)
        f = _ffn_dense(u2, p["ffn_w_gate"], p["ffn_w_up"], p["ffn_w_down"])
    return _ln2(x1, f, g2, ln[2], ln[3])


_NAMES = ("ada_w", "ada_b", "w_in", "gqa_q_norm", "gqa_k_norm", "mla_q_norm", "mla_w_uq", "mla_kv_norm", "mla_w_ukv",
          "mlstm_conv", "mlstm_gate_b", "mlstm_head_norm", "w_out", "ln1_g", "ln1_b")


def kernel(x, c, l0_ada_w, l0_ada_b, l0_w_in, l0_gqa_q_norm, l0_gqa_k_norm, l0_mla_q_norm, l0_mla_w_uq, l0_mla_kv_norm, l0_mla_w_ukv, l0_mlstm_conv, l0_mlstm_gate_b, l0_mlstm_head_norm, l0_w_out, l0_ln1_g, l0_ln1_b, l0_ffn_w_gate, l0_ffn_w_up, l0_ffn_w_down, l0_ln2_g, l0_ln2_b, l1_ada_w, l1_ada_b, l1_w_in, l1_gqa_q_norm, l1_gqa_k_norm, l1_mla_q_norm, l1_mla_w_uq, l1_mla_kv_norm, l1_mla_w_ukv, l1_mlstm_conv, l1_mlstm_gate_b, l1_mlstm_head_norm, l1_w_out, l1_ln1_g, l1_ln1_b, l1_router_w, l1_moe_w_gate, l1_moe_w_up, l1_moe_w_down, l1_ln2_g, l1_ln2_b):
    b, s, d = x.shape
    p0 = dict(zip(_NAMES, (l0_ada_w, l0_ada_b, l0_w_in, l0_gqa_q_norm, l0_gqa_k_norm, l0_mla_q_norm, l0_mla_w_uq,
                           l0_mla_kv_norm, l0_mla_w_ukv, l0_mlstm_conv, l0_mlstm_gate_b, l0_mlstm_head_norm, l0_w_out,
                           l0_ln1_g, l0_ln1_b)))
    p0.update(ffn_w_gate=l0_ffn_w_gate, ffn_w_up=l0_ffn_w_up, ffn_w_down=l0_ffn_w_down, ln2_g=l0_ln2_g, ln2_b=l0_ln2_b)
    p1 = dict(zip(_NAMES, (l1_ada_w, l1_ada_b, l1_w_in, l1_gqa_q_norm, l1_gqa_k_norm, l1_mla_q_norm, l1_mla_w_uq,
                           l1_mla_kv_norm, l1_mla_w_ukv, l1_mlstm_conv, l1_mlstm_gate_b, l1_mlstm_head_norm, l1_w_out,
                           l1_ln1_g, l1_ln1_b)))
    p1.update(router_w=l1_router_w, moe_w_gate=l1_moe_w_gate, moe_w_up=l1_moe_w_up, moe_w_down=l1_moe_w_down,
              ln2_g=l1_ln2_g, ln2_b=l1_ln2_b)
    tabs = _rope_tables(s)
    outs = []
    for bi in range(b):
        xb = x[bi]
        c_col = c[bi].reshape(d, 1)
        xb = _layer(xb, c_col, tabs, p0, moe=False)
        xb = _layer(xb, c_col, tabs, p1, moe=True)
        outs.append(xb)
    return jnp.stack(outs, axis=0)
```

```python
import functools

import jax
import jax.numpy as jnp
from jax import lax
from jax.experimental import pallas as pl
from jax.experimental.pallas import tpu as pltpu

F32 = jnp.float32
BF16 = jnp.bfloat16

GRID_W = 64
ROPE_THETA = 10000.0
HEAD64 = 64
GQA_HEADS = 16
GQA_KV = 4
MLA_HEADS = 4
MLA_NOPE = 128
MLA_V = 128
MLA_QR = 512
MLA_KVR = 256
ML_HEADS = 4
ML_D = 128
ML_W = ML_HEADS * ML_D
CHUNK = 128
N_EXPERTS = 8
RMS_EPS = 1e-6
LN_EPS = 1e-5
DEPTH = 2
ALPHA = (2 * DEPTH) ** 0.25

LANES = 128
NEG_BIG = -1e30
LOG2E = 1.4426950408889634
ONES_ROWS = 16
ATTN_BUFS = 4
ATTN_AHEAD = 1
VMEM_LIMIT = 56 * 1024 * 1024
FFN_TM = 1024
GATHER_ROWS_PER_STEP = 64


def _cp(*sem, vmem=VMEM_LIMIT):
    return pltpu.CompilerParams(dimension_semantics=sem, vmem_limit_bytes=vmem)


def _full(shape):
    n = len(shape)
    return pl.BlockSpec(shape, lambda *_: (0,) * n)


def _bf(x):
    return x.astype(BF16)


def _dot(a, b):
    return jnp.dot(a, b, preferred_element_type=F32)


def _split3(x):
    hi = _bf(x)
    r1 = x - hi.astype(F32)
    mid = _bf(r1)
    lo = _bf(r1 - mid.astype(F32))
    return hi, mid, lo


def _ada_kernel(c_ref, w_ref, b_ref, o_ref):
    c = c_ref[...]
    ca = c * jax.nn.sigmoid(c)
    o_ref[...] = jnp.sum(ca * w_ref[...], axis=0, keepdims=True) + b_ref[...]


def _ada(c_col, w, b):
    d, n = w.shape
    tn = 1024
    return pl.pallas_call(
        _ada_kernel,
        out_shape=jax.ShapeDtypeStruct((1, n), F32),
        grid=(n // tn,),
        in_specs=[_full((d, 1)), pl.BlockSpec((d, tn), lambda j: (0, j)), pl.BlockSpec((1, tn), lambda j: (0, j))],
        out_specs=pl.BlockSpec((1, tn), lambda j: (0, j)),
        compiler_params=_cp("parallel"),
        name="ada_mod",
    )(c_col, w, b.reshape(1, n))


def _group_sumsq(blk, bd):
    sq = blk * blk
    hi = _bf(sq)
    lo = _bf(sq - hi.astype(F32))
    return _dot(hi, bd) + _dot(lo, bd)


def _rope(blk, cos, sin_a, sin_b):
    return blk * cos + pltpu.roll(blk, LANES - 16, 1) * sin_a + pltpu.roll(blk, 16, 1) * sin_b


def _modulate(x, scale, shift):
    return x * (1.0 + scale) + shift


def _ones_rows(n):
    r = lax.broadcasted_iota(jnp.int32, (ONES_ROWS, n), 0)
    return jnp.where(r == 0, 1.0, 0.0).astype(BF16)


def _inproj_gqa_kernel(x_ref, sc_ref, sh_ref, w_ref, qg_ref, kg_ref, cos_ref, sa_ref, sb_ref, bd_ref,
                       qT_ref, k_ref, vT_ref):
    u = _bf(_modulate(x_ref[...], sc_ref[...], sh_ref[...]))
    y = _dot(u, w_ref[...])
    cos, sa, sb, bd = cos_ref[...], sa_ref[...], sb_ref[...], bd_ref[...]
    nq = GQA_HEADS * HEAD64 // LANES
    nk = GQA_KV * HEAD64 // LANES
    scale = HEAD64 ** -0.5 * LOG2E
    for j in range(nq):
        blk = y[:, j * LANES:(j + 1) * LANES]
        ss = _group_sumsq(blk, bd)
        blk = blk * lax.rsqrt(ss * (1.0 / HEAD64) + RMS_EPS) * qg_ref[...]
        blk = _rope(blk, cos, sa, sb) * scale
        qT_ref[j * LANES:(j + 1) * LANES, :] = _bf(blk.T)
    for j in range(nk):
        blk = y[:, (nq + j) * LANES:(nq + j + 1) * LANES]
        ss = _group_sumsq(blk, bd)
        blk = blk * lax.rsqrt(ss * (1.0 / HEAD64) + RMS_EPS) * kg_ref[...]
        blk = _bf(_rope(blk, cos, sa, sb))
        k_ref[2 * j] = blk[:, :HEAD64]
        k_ref[2 * j + 1] = blk[:, HEAD64:]
    vT = _bf(y[:, (nq + nk) * LANES:].T)
    ones = _ones_rows(vT.shape[1])
    dve = HEAD64 + ONES_ROWS
    for h in range(GQA_KV):
        vT_ref[h * dve:h * dve + HEAD64, :] = vT[h * HEAD64:(h + 1) * HEAD64, :]
        vT_ref[h * dve + HEAD64:(h + 1) * dve, :] = ones


def _inproj_gqa(x, sc, sh, w, qg, kg, cos, sa, sb, bd, tm=512):
    s, d = x.shape
    n = w.shape[1]
    row = lambda i: (i, 0)
    return pl.pallas_call(
        _inproj_gqa_kernel,
        out_shape=(jax.ShapeDtypeStruct((GQA_HEADS * HEAD64, s), BF16),
                   jax.ShapeDtypeStruct((GQA_KV, s, HEAD64), BF16),
                   jax.ShapeDtypeStruct((GQA_KV * (HEAD64 + ONES_ROWS), s), BF16)),
        grid=(s // tm,),
        in_specs=[pl.BlockSpec((tm, d), row), _full((1, d)), _full((1, d)), _full((d, n)),
                  _full((1, LANES)), _full((1, LANES)),
                  pl.BlockSpec((tm, LANES), row), pl.BlockSpec((tm, LANES), row), pl.BlockSpec((tm, LANES), row),
                  _full((LANES, LANES))],
        out_specs=(pl.BlockSpec((GQA_HEADS * HEAD64, tm), lambda i: (0, i)),
                   pl.BlockSpec((GQA_KV, tm, HEAD64), lambda i: (0, i, 0)),
                   pl.BlockSpec((GQA_KV * (HEAD64 + ONES_ROWS), tm), lambda i: (0, i))),
        compiler_params=_cp("parallel"),
        name="inproj_gqa",
    )(x, sc, sh, w, qg, kg, cos, sa, sb, bd)


def _inproj_mla_kernel(x_ref, sc_ref, sh_ref, w_ref, qn_ref, kvn_ref, wuq_ref, wukv_ref, cos_ref, sa_ref, sb_ref,
                       qT_ref, k_ref, vT_ref):
    u = _bf(_modulate(x_ref[...], sc_ref[...], sh_ref[...]))
    y = _dot(u, w_ref[...])
    cos, sa, sb = cos_ref[...], sa_ref[...], sb_ref[...]
    cq = y[:, :MLA_QR]
    ckv = y[:, MLA_QR:MLA_QR + MLA_KVR]
    kr = y[:, MLA_QR + MLA_KVR:]
    cqn = cq * lax.rsqrt(jnp.mean(cq * cq, axis=-1, keepdims=True) + RMS_EPS) * qn_ref[...]
    ckvn = ckv * lax.rsqrt(jnp.mean(ckv * ckv, axis=-1, keepdims=True) + RMS_EPS) * kvn_ref[...]
    qm = _dot(_bf(cqn), wuq_ref[...])
    kvm = _dot(_bf(ckvn), wukv_ref[...])
    scale = (MLA_NOPE + HEAD64) ** -0.5 * LOG2E
    dq = MLA_NOPE + HEAD64
    dve = MLA_V + ONES_ROWS
    ones = _ones_rows(y.shape[0])
    k_rope = _bf(_rope(kr, cos, sa, sb)[:, :HEAD64])
    for p in range(MLA_HEADS // 2):
        rblk = qm[:, MLA_HEADS * MLA_NOPE + p * LANES:MLA_HEADS * MLA_NOPE + (p + 1) * LANES]
        rT = _bf((_rope(rblk, cos, sa, sb) * scale).T)
        for e in range(2):
            h = 2 * p + e
            qT_ref[h * dq + MLA_NOPE:(h + 1) * dq, :] = rT[e * HEAD64:(e + 1) * HEAD64, :]
    for h in range(MLA_HEADS):
        nope = qm[:, h * MLA_NOPE:(h + 1) * MLA_NOPE] * scale
        qT_ref[h * dq:h * dq + MLA_NOPE, :] = _bf(nope.T)
        k_ref[h, :, :MLA_NOPE] = _bf(kvm[:, h * 2 * LANES:h * 2 * LANES + MLA_NOPE])
        k_ref[h, :, MLA_NOPE:] = k_rope
        vT_ref[h * dve:h * dve + MLA_V, :] = _bf(kvm[:, h * 2 * LANES + MLA_NOPE:(h + 1) * 2 * LANES].T)
        vT_ref[h * dve + MLA_V:(h + 1) * dve, :] = ones


def _inproj_mla(x, sc, sh, w, qn, kvn, wuq, wukv, cos, sa, sb, tm=512):
    s, d = x.shape
    n = w.shape[1]
    dq = MLA_NOPE + HEAD64
    row = lambda i: (i, 0)
    return pl.pallas_call(
        _inproj_mla_kernel,
        out_shape=(jax.ShapeDtypeStruct((MLA_HEADS * dq, s), BF16),
                   jax.ShapeDtypeStruct((MLA_HEADS, s, dq), BF16),
                   jax.ShapeDtypeStruct((MLA_HEADS * (MLA_V + ONES_ROWS), s), BF16)),
        grid=(s // tm,),
        in_specs=[pl.BlockSpec((tm, d), row), _full((1, d)), _full((1, d)), _full((d, n)),
                  _full((1, MLA_QR)), _full((1, MLA_KVR)), _full(wuq.shape), _full(wukv.shape),
                  pl.BlockSpec((tm, LANES), row), pl.BlockSpec((tm, LANES), row), pl.BlockSpec((tm, LANES), row)],
        out_specs=(pl.BlockSpec((MLA_HEADS * dq, tm), lambda i: (0, i)),
                   pl.BlockSpec((MLA_HEADS, tm, dq), lambda i: (0, i, 0)),
                   pl.BlockSpec((MLA_HEADS * (MLA_V + ONES_ROWS), tm), lambda i: (0, i))),
        compiler_params=_cp("parallel"),
        name="inproj_mla",
    )(x, sc, sh, w, qn, kvn, wuq, wukv, cos, sa, sb)


def _inproj_mlstm_kernel(x_ref, xp_ref, xn_ref, sc_ref, sh_ref, w_ref, conv_ref, gb_ref,
                         q_ref, kT_ref, v_ref, o_ref, gT_ref):
    i = pl.program_id(0)
    last = pl.num_programs(0) - 1
    sc, sh = sc_ref[...], sh_ref[...]
    u = _bf(_modulate(x_ref[...], sc, sh))
    y = _dot(u, w_ref[...])
    tm = y.shape[0]
    qk = y[:, :2 * ML_W]
    halo = jnp.concatenate([xp_ref[...], xn_ref[...]], axis=0)
    hy = _dot(_bf(_modulate(halo, sc, sh)), w_ref[:, :2 * ML_W])
    prev_row = jnp.where(i == 0, 0.0, hy[7:8, :])
    next_row = jnp.where(i == last, 0.0, hy[8:9, :])
    rows = lax.broadcasted_iota(jnp.int32, qk.shape, 0)
    prev = jnp.where(rows == 0, prev_row, pltpu.roll(qk, 1, 0))
    nxt = jnp.where(rows == tm - 1, next_row, pltpu.roll(qk, tm - 1, 0))
    cw = conv_ref[...]
    z = cw[0:1, :] * prev + cw[1:2, :] * qk + cw[2:3, :] * nxt
    z = z * jax.nn.sigmoid(z)
    q_ref[...] = _bf(z[:, :ML_W])
    kT_ref[...] = (z[:, ML_W:] * (ML_D ** -0.5)).T
    v_ref[...] = _bf(y[:, 2 * ML_W:3 * ML_W])
    o_ref[...] = y[:, 3 * ML_W:4 * ML_W]
    g = y[:, 4 * ML_W:] + gb_ref[...]
    gT_ref[...] = g.T[:4 * ML_HEADS, :]


def _inproj_mlstm(x, sc, sh, w, conv, gb, tm=512):
    s, d = x.shape
    n = w.shape[1]
    row = lambda i: (i, 0)
    col = lambda i: (0, i)
    r8 = tm // 8
    return pl.pallas_call(
        _inproj_mlstm_kernel,
        out_shape=(jax.ShapeDtypeStruct((s, ML_W), BF16), jax.ShapeDtypeStruct((ML_W, s), F32),
                   jax.ShapeDtypeStruct((s, ML_W), BF16), jax.ShapeDtypeStruct((s, ML_W), F32),
                   jax.ShapeDtypeStruct((4 * ML_HEADS, s), F32)),
        grid=(s // tm,),
        in_specs=[pl.BlockSpec((tm, d), row),
                  pl.BlockSpec((8, d), lambda i: (jnp.maximum(i * r8 - 1, 0), 0)),
                  pl.BlockSpec((8, d), lambda i: (jnp.minimum((i + 1) * r8, s // 8 - 1), 0)),
                  _full((1, d)), _full((1, d)), _full((d, n)), _full((3, 2 * ML_W)), _full((1, LANES))],
        out_specs=(pl.BlockSpec((tm, ML_W), row), pl.BlockSpec((ML_W, tm), col),
                   pl.BlockSpec((tm, ML_W), row), pl.BlockSpec((tm, ML_W), row),
                   pl.BlockSpec((4 * ML_HEADS, tm), col)),
        compiler_params=_cp("parallel"),
        name="inproj_mlstm",
    )(x, x, x, sc, sh, w, conv, gb)


def _attn_kernel(qT_ref, k_ref, vT_ref, o_ref, s_sc, acc_sc, *, groups, dq, dv, tk):
    hp = k_ref.shape[0]
    seq = k_ref.shape[1]
    n = seq // tk
    dve = dv + ONES_ROWS
    cq = qT_ref.shape[1]
    nq = groups * cq
    qTs = []
    for hh in range(hp):
        r0 = hh * groups * dq
        qTs.append(jnp.concatenate([qT_ref[r0 + g * dq:r0 + (g + 1) * dq, :] for g in range(groups)], axis=1)
                   if groups > 1 else qT_ref[r0:r0 + dq, :])
    acc_sc[...] = jnp.zeros(acc_sc.shape, F32)

    def scores(hh, c, buf):
        off = c * tk if isinstance(c, int) else pl.multiple_of(c * tk, tk)
        s = _dot(k_ref[hh, pl.ds(off, tk), :], qTs[hh])
        s_sc[hh, buf] = s
        return jnp.max(s, axis=0, keepdims=True)

    def accumulate(hh, c, buf, mx, m_old):
        off = c * tk if isinstance(c, int) else pl.multiple_of(c * tk, tk)
        m_new = jnp.maximum(m_old, mx)
        p = jnp.exp2(_bf(s_sc[hh, buf] - m_new))
        pv = _dot(vT_ref[hh * dve:(hh + 1) * dve, pl.ds(off, tk)], p)
        acc_sc[hh] = jnp.exp2(m_old - m_new) * acc_sc[hh] + pv
        return m_new

    def body(base, carry, tail=False):
        mxs, ms = [list(t) for t in carry[0]], list(carry[1])
        for j in range(ATTN_BUFS):
            for hh in range(hp):
                if not (tail and j + ATTN_AHEAD >= ATTN_BUFS):
                    mxs[hh].append(scores(hh, base + j + ATTN_AHEAD, (j + ATTN_AHEAD) % ATTN_BUFS))
                ms[hh] = accumulate(hh, base + j, j, mxs[hh].pop(0), ms[hh])
        return tuple(tuple(t) for t in mxs), tuple(ms)

    carry = (tuple(tuple(scores(hh, c, c) for c in range(ATTN_AHEAD)) for hh in range(hp)),
             tuple(jnp.full((1, nq), NEG_BIG, F32) for _ in range(hp)))
    carry = lax.fori_loop(0, n // ATTN_BUFS - 1, lambda i, cr: body(i * ATTN_BUFS, cr), carry)
    body(n - ATTN_BUFS, carry, tail=True)
    for hh in range(hp):
        oT = acc_sc[hh, :dv, :] * (1.0 / acc_sc[hh, dv:dv + 1, :])
        if groups > 1:
            oT = jnp.concatenate([oT[:, g * cq:(g + 1) * cq] for g in range(groups)], axis=0)
        o_ref[:, hh * groups * dv:(hh + 1) * groups * dv] = _bf(oT.T)


def _attention(qT, k, vT, *, groups, dq, dv, cq, hp, tk=512):
    h, s, _ = k.shape
    nq = groups * cq
    dve = dv + ONES_ROWS
    assert (s // tk) % ATTN_BUFS == 0 and h % hp == 0
    kern = functools.partial(_attn_kernel, groups=groups, dq=dq, dv=dv, tk=tk)
    return pl.pallas_call(
        kern,
        out_shape=jax.ShapeDtypeStruct((s, h * groups * dv), BF16),
        grid=(h // hp, s // cq),
        in_specs=[pl.BlockSpec((hp * groups * dq, cq), lambda a, i: (a, i)),
                  pl.BlockSpec((hp, s, dq), lambda a, i: (a, 0, 0)),
                  pl.BlockSpec((hp * dve, s), lambda a, i: (a, 0))],
        out_specs=pl.BlockSpec((cq, hp * groups * dv), lambda a, i: (i, a)),
        scratch_shapes=[pltpu.VMEM((hp, ATTN_BUFS, tk, nq), F32), pltpu.VMEM((hp, dve, nq), F32)],
        compiler_params=_cp("parallel", "parallel"),
        name="attention_g%d" % groups,
    )(qT, k, vT)


def _mlstm_kernel(q_ref, kT_ref, v_ref, o_ref, g_ref, hn_ref, out_ref,
                  cn_sc, mprev_sc, w_sc, r_sc, lf_sc, g_sc, mloc_sc, hacc_sc):
    head = pl.program_id(0)
    nc = g_ref.shape[1]
    L = CHUNK
    ii = lax.broadcasted_iota(jnp.int32, (L, L), 0)
    jj = lax.broadcasted_iota(jnp.int32, (L, L), 1)
    e0 = jnp.where(lax.broadcasted_iota(jnp.int32, (L, LANES), 1) == 0, 1.0, 0.0).astype(BF16)

    for direction in range(2):
        fw = direction == 0
        mask = (jj <= ii) if fw else (jj >= ii)
        tri = jnp.where((ii <= jj) if fw else (ii >= jj), 1.0, 0.0).astype(BF16)
        li = g_ref[2 * ML_HEADS * direction + head]
        fg = g_ref[2 * ML_HEADS * direction + ML_HEADS + head]
        lf = jnp.minimum(fg, 0.0) - jnp.log1p(jnp.exp(-jnp.abs(fg)))
        hi, mid, lo = _split3(lf)
        b = _dot(hi, tri) + _dot(mid, tri) + _dot(lo, tri)
        gtot = jnp.sum(lf, axis=-1, keepdims=True)
        a = gtot - b + li
        mloc = jnp.max(a, axis=-1, keepdims=True)
        w_sc[...] = jnp.exp(a - mloc)
        r_sc[...] = li - b
        lf_sc[...] = lf
        g_sc[...] = jnp.broadcast_to(gtot, (nc, LANES))
        mloc_sc[...] = jnp.broadcast_to(mloc, (nc, LANES))

        def scan_body(i, carry, fw=fw):
            cn, m = carry
            cc = i if fw else nc - 1 - i
            off = pl.multiple_of(cc * L, L)
            cn_sc[cc] = _bf(cn)
            mprev_sc[pl.ds(cc, 1), :] = jnp.broadcast_to(m, (1, LANES))
            kw = _bf(kT_ref[:, pl.ds(off, L)] * w_sc[pl.ds(cc, 1), :])
            vext = jnp.concatenate([v_ref[pl.ds(off, L), :], e0], axis=1)
            kv = _dot(kw, vext)
            g = g_sc[pl.ds(cc, 1), 0:1]
            ml = mloc_sc[pl.ds(cc, 1), 0:1]
            m_new = jnp.maximum(g + m, ml)
            cn = jnp.exp(g + m - m_new) * cn + jnp.exp(ml - m_new) * kv
            return cn, m_new

        lax.fori_loop(0, nc, scan_body, (jnp.zeros((ML_D, 2 * ML_D), F32), jnp.zeros((1, 1), F32)), unroll=4)

        def out_body(cc, carry, fw=fw, mask=mask):
            off = pl.multiple_of(cc * L, L)
            mp = mprev_sc[pl.ds(cc, 1), 0:1]
            rm = jnp.where(mask, r_sc[pl.ds(cc, 1), :], -jnp.inf)
            mt = jnp.maximum(mp, jnp.max(rm, axis=-1, keepdims=True))
            d = jnp.exp(rm - mt)
            iw = jnp.exp(mp - mt)
            qc = q_ref[pl.ds(off, L), :]
            qk = _dot(qc, _bf(kT_ref[:, pl.ds(off, L)])) * d
            num_i = _dot(_bf(qk), v_ref[pl.ds(off, L), :])
            den_i = jnp.sum(qk, axis=-1, keepdims=True)
            inter = _dot(qc, cn_sc[cc])
            num = num_i + iw * inter[:, :ML_D]
            den = den_i + iw * inter[:, ML_D:ML_D + 1]
            bcol = jnp.sum(jnp.where(mask, lf_sc[pl.ds(cc, 1), :], 0.0), axis=-1, keepdims=True)
            h = num / jnp.maximum(jnp.abs(den), jnp.exp(-(bcol + mt)))
            if fw:
                hacc_sc[pl.ds(off, L), :] = h
            else:
                h = h + hacc_sc[pl.ds(off, L), :]
                mu = jnp.mean(h, axis=-1, keepdims=True)
                hc = h - mu
                var = jnp.mean(hc * hc, axis=-1, keepdims=True)
                hn = hc * lax.rsqrt(var + LN_EPS) * hn_ref[...]
                out_ref[pl.ds(off, L), :] = _bf(jax.nn.sigmoid(o_ref[pl.ds(off, L), :]) * hn)
            return carry

        lax.fori_loop(0, nc, out_body, 0, unroll=4)


def _mlstm(q, kT, v, o, g3, hn):
    s = q.shape[0]
    nc = s // CHUNK
    colblk = lambda h: (0, h)
    return pl.pallas_call(
        _mlstm_kernel,
        out_shape=jax.ShapeDtypeStruct((s, ML_W), BF16),
        grid=(ML_HEADS,),
        in_specs=[pl.BlockSpec((s, ML_D), colblk), pl.BlockSpec((ML_D, s), lambda h: (h, 0)),
                  pl.BlockSpec((s, ML_D), colblk), pl.BlockSpec((s, ML_D), colblk),
                  _full(g3.shape), pl.BlockSpec((1, ML_D), colblk)],
        out_specs=pl.BlockSpec((s, ML_D), colblk),
        scratch_shapes=[pltpu.VMEM((nc, ML_D, 2 * ML_D), BF16), pltpu.VMEM((nc, LANES), F32),
                        pltpu.VMEM((nc, CHUNK), F32), pltpu.VMEM((nc, CHUNK), F32), pltpu.VMEM((nc, CHUNK), F32),
                        pltpu.VMEM((nc, LANES), F32), pltpu.VMEM((nc, LANES), F32),
                        pltpu.VMEM((s, ML_D), F32)],
        compiler_params=_cp("parallel"),
        name="mlstm",
    )(q, kT, v, o, g3, hn)


def _layer_norm(z, g, b):
    mu = jnp.mean(z, axis=-1, keepdims=True)
    zc = z - mu
    var = jnp.mean(zc * zc, axis=-1, keepdims=True)
    return zc * lax.rsqrt(var + LN_EPS) * g + b


def _outproj_kernel(*refs, route):
    if route:
        (x_ref, a_ref, b_ref, c_ref, w_ref, gate_ref, g_ref, bb_ref, sc_ref, sh_ref, rw_both_ref, rw_high_ref,
         x1_ref, u2_ref, lg_ref) = refs
    else:
        x_ref, a_ref, b_ref, c_ref, w_ref, gate_ref, g_ref, bb_ref, sc_ref, sh_ref, x1_ref, u2_ref = refs
    na, nb = a_ref.shape[1], b_ref.shape[1]
    y = _dot(a_ref[...], w_ref[:na, :]) + _dot(b_ref[...], w_ref[na:na + nb, :]) + _dot(c_ref[...], w_ref[na + nb:, :])
    x1 = _layer_norm(ALPHA * x_ref[...] + gate_ref[...] * y, g_ref[...], bb_ref[...])
    x1_ref[...] = x1
    u2 = _modulate(x1, sc_ref[...], sh_ref[...])
    u2_ref[...] = u2.astype(u2_ref.dtype)
    if route:
        hi = _bf(u2)
        lo = _bf(u2 - hi.astype(F32))
        both = _dot(hi, rw_both_ref[...])
        lg_ref[...] = both[:, :LANES] + (_dot(lo, rw_high_ref[...]) + both[:, LANES:])


def _outproj(x, a, b, c, w, gate, g, bb, sc, sh, rw=None, tm=512):
    s, d = x.shape
    row = lambda i: (i, 0)
    route = rw is not None
    ins = [x, a, b, c, w, gate, g, bb, sc, sh]
    in_specs = [pl.BlockSpec((tm, d), row), pl.BlockSpec((tm, a.shape[1]), row), pl.BlockSpec((tm, b.shape[1]), row),
                pl.BlockSpec((tm, c.shape[1]), row), _full(w.shape)] + [_full((1, d))] * 5
    out_shape = [jax.ShapeDtypeStruct((s, d), F32), jax.ShapeDtypeStruct((s, d), F32 if route else BF16)]
    out_specs = [pl.BlockSpec((tm, d), row), pl.BlockSpec((tm, d), row)]
    if route:
        rwp = jnp.zeros((d, LANES), F32).at[:, :rw.shape[1]].set(rw)
        rwh = _bf(rwp)
        rwl = _bf(rwp - rwh.astype(F32))
        ins += [jnp.concatenate([rwh, rwl], axis=1), rwh]
        in_specs += [_full((d, 2 * LANES)), _full((d, LANES))]
        out_shape.append(jax.ShapeDtypeStruct((s, LANES), F32))
        out_specs.append(pl.BlockSpec((tm, LANES), row))
    return pl.pallas_call(
        functools.partial(_outproj_kernel, route=route),
        out_shape=tuple(out_shape),
        grid=(s // tm,),
        in_specs=in_specs,
        out_specs=tuple(out_specs),
        compiler_params=_cp("parallel"),
        name="outproj_ln1",
    )(*ins)


def _swiglu_partial(x, wg, wu, wd):
    gt = _dot(x, _bf(wg))
    up = _dot(x, _bf(wu))
    return _dot(_bf(gt * jax.nn.sigmoid(gt) * up), _bf(wd))


def _ffn_dense_kernel(x_ref, wg_ref, wu_ref, wd_ref, y_ref):
    j = pl.program_id(1)

    @pl.when(j == 0)
    def _():
        y_ref[...] = _swiglu_partial(x_ref[...], wg_ref[...], wu_ref[...], wd_ref[...])

    @pl.when(j > 0)
    def _():
        y_ref[...] += _swiglu_partial(x_ref[...], wg_ref[...], wu_ref[...], wd_ref[...])


def _ffn_dense(x, wg, wu, wd, tm=FFN_TM, tf=256):
    s, d = x.shape
    f = wg.shape[1]
    return pl.pallas_call(
        _ffn_dense_kernel,
        out_shape=jax.ShapeDtypeStruct((s, d), F32),
        grid=(s // tm, f // tf),
        in_specs=[pl.BlockSpec((tm, d), lambda i, j: (i, 0)), pl.BlockSpec((d, tf), lambda i, j: (0, j)),
                  pl.BlockSpec((d, tf), lambda i, j: (0, j)), pl.BlockSpec((tf, d), lambda i, j: (j, 0))],
        out_specs=pl.BlockSpec((tm, d), lambda i, j: (i, 0)),
        compiler_params=_cp("parallel", "arbitrary"),
        name="swiglu_dense",
    )(x, wg, wu, wd)


def _ffn_routed_kernel(te_ref, nu_ref, src_ref, u_hbm, wg_ref, wu_ref, wd_ref, y_ref, xf_buf, xb_buf, sem):
    i, j = pl.program_id(0), pl.program_id(1)
    n_live = nu_ref[0]
    live = i < n_live
    tm = y_ref.shape[0]

    def row_copy(tile, r):
        return pltpu.make_async_copy(u_hbm.at[pl.ds(src_ref[tile * tm + r], 1)], xf_buf.at[pl.ds(r, 1)], sem)

    def issue(tile, first, count):
        def body(r, carry):
            row_copy(tile, first + r).start()
            return carry
        lax.fori_loop(0, count, body, 0, unroll=8)

    @pl.when((i == 0) & (j == 0))
    def _():
        issue(0, 0, tm)

    @pl.when(live & (j == 0))
    def _():
        pltpu.make_async_copy(u_hbm.at[pl.ds(0, tm)], xf_buf, sem).wait()
        xb_buf[...] = _bf(xf_buf[...])
        y_ref[...] = _swiglu_partial(xb_buf[...], wg_ref[0], wu_ref[0], wd_ref[0])

    @pl.when((j >= 1) & (j <= tm // GATHER_ROWS_PER_STEP) & (i + 1 < n_live))
    def _():
        issue(i + 1, (j - 1) * GATHER_ROWS_PER_STEP, GATHER_ROWS_PER_STEP)

    @pl.when(live & (j > 0))
    def _():
        y_ref[...] += _swiglu_partial(xb_buf[...], wg_ref[0], wu_ref[0], wd_ref[0])

    @pl.when(jnp.logical_not(live) & (j == 0))
    def _():
        y_ref[...] = jnp.zeros(y_ref.shape, F32)


def _ffn_routed(u, src, wg, wu, wd, tile_expert, n_used, tm, tf=256):
    s, d = u.shape
    f = wg.shape[2]
    nt, nf = src.shape[0] // tm, f // tf
    assert tm % GATHER_ROWS_PER_STEP == 0 and nf > tm // GATHER_ROWS_PER_STEP

    def clamp(i, j, nu):
        live = i < nu[0]
        return jnp.where(live, i, nu[0] - 1), jnp.where(live, j, nf - 1)

    def gmap(i, j, te, nu, sr):
        ii, jj = clamp(i, j, nu)
        return te[ii], 0, jj

    def dmap(i, j, te, nu, sr):
        ii, jj = clamp(i, j, nu)
        return te[ii], jj, 0

    return pl.pallas_call(
        _ffn_routed_kernel,
        out_shape=jax.ShapeDtypeStruct((nt * tm, d), F32),
        grid_spec=pltpu.PrefetchScalarGridSpec(
            num_scalar_prefetch=3,
            grid=(nt, nf),
            in_specs=[pl.BlockSpec(memory_space=pl.ANY), pl.BlockSpec((1, d, tf), gmap), pl.BlockSpec((1, d, tf), gmap),
                      pl.BlockSpec((1, tf, d), dmap)],
            out_specs=pl.BlockSpec((tm, d), lambda i, j, te, nu, sr: (i, 0)),
            scratch_shapes=[pltpu.VMEM((tm, d), F32), pltpu.VMEM((tm, d), BF16), pltpu.SemaphoreType.DMA(())]),
        compiler_params=_cp("arbitrary", "arbitrary"),
        name="swiglu_routed",
    )(tile_expert, n_used, src, u, wg, wu, wd)


def _ln2_kernel(x_ref, f_ref, gate_ref, g_ref, b_ref, o_ref):
    o_ref[...] = _layer_norm(ALPHA * x_ref[...] + gate_ref[...] * f_ref[...], g_ref[...], b_ref[...])


def _ln2(x, f, gate, g, b, tm=512):
    s, d = x.shape
    row = lambda i: (i, 0)
    return pl.pallas_call(
        _ln2_kernel,
        out_shape=jax.ShapeDtypeStruct((s, d), F32),
        grid=(s // tm,),
        in_specs=[pl.BlockSpec((tm, d), row), pl.BlockSpec((tm, d), row)] + [_full((1, d))] * 3,
        out_specs=pl.BlockSpec((tm, d), row),
        compiler_params=_cp("parallel"),
        name="residual_ln2",
    )(x, f, gate, g, b)


def _row_copy(y_hbm, idx_ref, buf, sem, tile, slot, r, tm):
    return pltpu.make_async_copy(y_hbm.at[pl.ds(idx_ref[tile * tm + r], 1)], buf.at[slot, pl.ds(r, 1)], sem.at[slot])


def _ln2_combine_kernel(d1_ref, d2_ref, x_ref, info_ref, y_hbm, gate_ref, g_ref, b_ref, o_ref, a_buf, b_buf, sem):
    i = pl.program_id(0)
    n = pl.num_programs(0)
    tm = x_ref.shape[0]

    def issue(tile, slot):
        def body(r, carry):
            _row_copy(y_hbm, d1_ref, a_buf, sem, tile, slot, r, tm).start()
            _row_copy(y_hbm, d2_ref, b_buf, sem, tile, slot, r, tm).start()
            return carry
        lax.fori_loop(0, tm, body, 0, unroll=8)

    @pl.when(i == 0)
    def _():
        issue(0, 0)

    @pl.when(i + 1 < n)
    def _():
        issue(i + 1, (i + 1) % 2)

    slot = i % 2
    pltpu.make_async_copy(y_hbm.at[pl.ds(0, tm)], a_buf.at[slot], sem.at[slot]).wait()
    pltpu.make_async_copy(y_hbm.at[pl.ds(0, tm)], b_buf.at[slot], sem.at[slot]).wait()
    info = info_ref[...]
    f = info[:, 2:3] * a_buf[slot] + info[:, 3:4] * b_buf[slot]
    o_ref[...] = _layer_norm(ALPHA * x_ref[...] + gate_ref[...] * f, g_ref[...], b_ref[...])


def _ln2_combine(x, info, y, d1, d2, gate, g, b, tm=256):
    s, d = x.shape
    row = lambda i, *_: (i, 0)
    one = lambda i, *_: (0, 0)
    return pl.pallas_call(
        _ln2_combine_kernel,
        out_shape=jax.ShapeDtypeStruct((s, d), F32),
        grid_spec=pltpu.PrefetchScalarGridSpec(
            num_scalar_prefetch=2,
            grid=(s // tm,),
            in_specs=[pl.BlockSpec((tm, d), row), pl.BlockSpec((tm, LANES), row), pl.BlockSpec(memory_space=pl.ANY),
                      pl.BlockSpec((1, d), one), pl.BlockSpec((1, d), one), pl.BlockSpec((1, d), one)],
            out_specs=pl.BlockSpec((tm, d), row),
            scratch_shapes=[pltpu.VMEM((2, tm, d), F32), pltpu.VMEM((2, tm, d), F32), pltpu.SemaphoreType.DMA((2,))]),
        compiler_params=_cp("arbitrary"),
        name="combine_ln2",
    )(d1, d2, x, info, y, gate, g, b)


def _router_kernel(lg_ref, info_ref, cnt_ref, carry_sc):
    i = pl.program_id(0)

    @pl.when(i == 0)
    def _():
        carry_sc[...] = jnp.zeros(carry_sc.shape, F32)

    lg = lg_ref[...]
    tm = lg.shape[0]
    lane = lax.broadcasted_iota(jnp.int32, lg.shape, 1)
    lg = jnp.where(lane < N_EXPERTS, lg, -jnp.inf)
    v1 = jnp.max(lg, axis=-1, keepdims=True)
    i1 = jnp.min(jnp.where(lg == v1, lane, LANES), axis=-1, keepdims=True)
    lg2 = jnp.where(lane == i1, -jnp.inf, lg)
    v2 = jnp.max(lg2, axis=-1, keepdims=True)
    i2 = jnp.min(jnp.where(lg2 == v2, lane, LANES), axis=-1, keepdims=True)
    e21 = jnp.exp(v2 - v1)
    w1 = 1.0 / (1.0 + e21)
    w2 = e21 * w1
    oh1 = lane == i1
    oh2 = lane == i2
    oh = jnp.where(oh1 | oh2, 1.0, 0.0)
    rr = lax.broadcasted_iota(jnp.int32, (tm, tm), 0)
    cc = lax.broadcasted_iota(jnp.int32, (tm, tm), 1)
    strict = jnp.where(cc < rr, 1.0, 0.0).astype(BF16)
    before = _dot(strict, _bf(oh)) + carry_sc[...]
    r1 = jnp.sum(jnp.where(oh1, before, 0.0), axis=-1, keepdims=True)
    r2 = jnp.sum(jnp.where(oh2, before, 0.0), axis=-1, keepdims=True)
    vals = (i1.astype(F32), i2.astype(F32), w1, w2, r1, r2)
    info = jnp.zeros(lg.shape, F32)
    for n, v in enumerate(vals):
        info = jnp.where(lane == n, v, info)
    info_ref[...] = info
    carry_sc[...] += jnp.sum(oh, axis=0, keepdims=True)
    cnt_ref[...] = carry_sc[...]


def _router(logits, tm=512):
    s = logits.shape[0]
    return pl.pallas_call(
        _router_kernel,
        out_shape=(jax.ShapeDtypeStruct((s, LANES), F32), jax.ShapeDtypeStruct((1, LANES), F32)),
        grid=(s // tm,),
        in_specs=[pl.BlockSpec((tm, LANES), lambda i: (i, 0))],
        out_specs=(pl.BlockSpec((tm, LANES), lambda i: (i, 0)), _full((1, LANES))),
        scratch_shapes=[pltpu.VMEM((1, LANES), F32)],
        compiler_params=_cp("arbitrary"),
        name="router_top2",
    )(logits)


def _rope_tables(s):
    rows = s // GRID_W
    n_freq = HEAD64 // 4
    inv_freq = ROPE_THETA ** (-jnp.arange(n_freq, dtype=F32) / n_freq)
    lane = jnp.arange(LANES)
    freq = inv_freq[lane % n_freq]
    ang_r = jnp.arange(rows, dtype=F32)[:, None] * freq
    ang_c = jnp.arange(GRID_W, dtype=F32)[:, None] * freq
    by_row = (lane % HEAD64) < HEAD64 // 2

    def table(fn):
        full = jnp.where(by_row, fn(ang_r)[:, None, :], fn(ang_c)[None, :, :])
        return full.reshape(s, LANES)

    cos, sin = table(jnp.cos), table(jnp.sin)
    first_half = (jnp.arange(LANES) % 32) < 16
    return cos, jnp.where(first_half, -sin, 0.0), jnp.where(first_half, 0.0, sin)


def _moe_ffn(u2, logits, wg, wu, wd, tm):
    s, d = u2.shape
    e = wg.shape[0]
    info, counts = _router(logits)
    e1, e2 = info[:, 0].astype(jnp.int32), info[:, 1].astype(jnp.int32)
    w1, w2 = info[:, 2], info[:, 3]
    r1, r2 = info[:, 4].astype(jnp.int32), info[:, 5].astype(jnp.int32)
    cnt = counts[0, :e].astype(jnp.int32)
    tiles = (cnt + tm - 1) // tm
    tile_end = jnp.cumsum(tiles)
    start = (tile_end - tiles) * tm
    nt = (2 * s) // tm + e
    n_used = tile_end[-1:].astype(jnp.int32)
    tile_expert = jnp.sum((jnp.arange(nt)[:, None] >= tile_end[None, :]).astype(jnp.int32), axis=1)
    tile_expert = jnp.minimum(tile_expert, e - 1)
    d1, d2 = start[e1] + r1, start[e2] + r2
    tok = jnp.arange(s, dtype=jnp.int32)
    src = jnp.zeros((nt * tm,), jnp.int32).at[jnp.concatenate([d1, d2])].set(jnp.concatenate([tok, tok]))
    y = _ffn_routed(u2, src, wg, wu, wd, tile_expert, n_used, tm=tm)
    return info, y, d1, d2


def _layer(x, c_col, tabs, p, moe):
    s, d = x.shape
    cos, sa, sb = tabs
    mod = _ada(c_col, p["ada_w"], p["ada_b"])
    sh1, sc1, g1, sh2, sc2, g2 = [mod[:, i * d:(i + 1) * d] for i in range(6)]

    w_in = p["w_in"]
    n_gqa = (GQA_HEADS + 2 * GQA_KV) * HEAD64
    n_mla = MLA_QR + MLA_KVR + HEAD64
    w_gqa = _bf(w_in[:, :n_gqa])
    w_mla = _bf(jnp.pad(w_in[:, n_gqa:n_gqa + n_mla], ((0, 0), (0, LANES - HEAD64))))
    w_ml = _bf(jnp.pad(w_in[:, n_gqa + n_mla:], ((0, 0), (0, LANES - 4 * ML_HEADS))))

    ii = jnp.arange(LANES)
    bd = _bf((ii[:, None] // HEAD64 == ii[None, :] // HEAD64).astype(F32))
    qg = jnp.tile(p["gqa_q_norm"], 2).reshape(1, LANES)
    kg = jnp.tile(p["gqa_k_norm"], 2).reshape(1, LANES)
    qT, k, vT = _inproj_gqa(x, sc1, sh1, w_gqa, qg, kg, cos, sa, sb, bd)
    out_a = _attention(qT, k, vT, groups=GQA_HEADS // GQA_KV, dq=HEAD64, dv=HEAD64, cq=128, hp=4)

    dq = MLA_NOPE + HEAD64
    wuq = p["mla_w_uq"].reshape(MLA_QR, MLA_HEADS, dq)
    wuq = _bf(jnp.concatenate([wuq[:, :, :MLA_NOPE].reshape(MLA_QR, -1), wuq[:, :, MLA_NOPE:].reshape(MLA_QR, -1)], axis=1))
    qTm, km, vTm = _inproj_mla(x, sc1, sh1, w_mla, p["mla_q_norm"].reshape(1, -1), p["mla_kv_norm"].reshape(1, -1),
                               wuq, _bf(p["mla_w_ukv"]), cos, sa, sb)
    out_b = _attention(qTm, km, vTm, groups=1, dq=dq, dv=MLA_V, cq=512, hp=2)

    gb = jnp.pad(p["mlstm_gate_b"], (0, LANES - 4 * ML_HEADS)).reshape(1, LANES)
    mq, mkT, mv, mo, gT = _inproj_mlstm(x, sc1, sh1, w_ml, p["mlstm_conv"], gb)
    out_c = _mlstm(mq, mkT, mv, mo, gT.reshape(4 * ML_HEADS, s // CHUNK, CHUNK), p["mlstm_head_norm"].reshape(1, -1))

    ln = [p[n].reshape(1, d) for n in ("ln1_g", "ln1_b", "ln2_g", "ln2_b")]
    if moe:
        x1, u2, logits = _outproj(x, out_a, out_b, out_c, _bf(p["w_out"]), g1, ln[0], ln[1], sc2, sh2, rw=p["router_w"])
        info, y, d1, d2 = _moe_ffn(u2, logits, p["moe_w_gate"], p["moe_w_up"], p["moe_w_down"], tm=FFN_TM)
        return _ln2_combine(x1, info, y, d1, d2, g2, ln[2], ln[3])
    else:
        x1, u2 = _outproj(x, out_a, out_b, out_c, _bf(p["w_out"]), g1, ln[0], ln[1], sc2, sh2)
        f = _ffn_dense(u2, p["ffn_w_gate"], p["ffn_w_up"], p["ffn_w_down"])
    return _ln2(x1, f, g2, ln[2], ln[3])


_NAMES = ("ada_w", "ada_b", "w_in", "gqa_q_norm", "gqa_k_norm", "mla_q_norm", "mla_w_uq", "mla_kv_norm", "mla_w_ukv",
          "mlstm_conv", "mlstm_gate_b", "mlstm_head_norm", "w_out", "ln1_g", "ln1_b")


def kernel(x, c, l0_ada_w, l0_ada_b, l0_w_in, l0_gqa_q_norm, l0_gqa_k_norm, l0_mla_q_norm, l0_mla_w_uq, l0_mla_kv_norm, l0_mla_w_ukv, l0_mlstm_conv, l0_mlstm_gate_b, l0_mlstm_head_norm, l0_w_out, l0_ln1_g, l0_ln1_b, l0_ffn_w_gate, l0_ffn_w_up, l0_ffn_w_down, l0_ln2_g, l0_ln2_b, l1_ada_w, l1_ada_b, l1_w_in, l1_gqa_q_norm, l1_gqa_k_norm, l1_mla_q_norm, l1_mla_w_uq, l1_mla_kv_norm, l1_mla_w_ukv, l1_mlstm_conv, l1_mlstm_gate_b, l1_mlstm_head_norm, l1_w_out, l1_ln1_g, l1_ln1_b, l1_router_w, l1_moe_w_gate, l1_moe_w_up, l1_moe_w_down, l1_ln2_g, l1_ln2_b):
    b, s, d = x.shape
    p0 = dict(zip(_NAMES, (l0_ada_w, l0_ada_b, l0_w_in, l0_gqa_q_norm, l0_gqa_k_norm, l0_mla_q_norm, l0_mla_w_uq,
                           l0_mla_kv_norm, l0_mla_w_ukv, l0_mlstm_conv, l0_mlstm_gate_b, l0_mlstm_head_norm, l0_w_out,
                           l0_ln1_g, l0_ln1_b)))
    p0.update(ffn_w_gate=l0_ffn_w_gate, ffn_w_up=l0_ffn_w_up, ffn_w_down=l0_ffn_w_down, ln2_g=l0_ln2_g, ln2_b=l0_ln2_b)
    p1 = dict(zip(_NAMES, (l1_ada_w, l1_ada_b, l1_w_in, l1_gqa_q_norm, l1_gqa_k_norm, l1_mla_q_norm, l1_mla_w_uq,
                           l1_mla_kv_norm, l1_mla_w_ukv, l1_mlstm_conv, l1_mlstm_gate_b, l1_mlstm_head_norm, l1_w_out,
                           l1_ln1_g, l1_ln1_b)))
    p1.update(router_w=l1_router_w, moe_w_gate=l1_moe_w_gate, moe_w_up=l1_moe_w_up, moe_w_down=l1_moe_w_down,
              ln2_g=l1_ln2_g, ln2_b=l1_ln2_b)
    tabs = _rope_tables(s)
    outs = []
    for bi in range(b):
        xb = x[bi]
        c_col = c[bi].reshape(d, 1)
        xb = _layer(xb, c_col, tabs, p0, moe=False)
        xb = _layer(xb, c_col, tabs, p1, moe=True)
        outs.append(xb)
    return jnp.stack(outs, axis=0)
```

```python
import functools

import jax
import jax.numpy as jnp
from jax import lax
from jax.experimental import pallas as pl
from jax.experimental.pallas import tpu as pltpu

F32 = jnp.float32
BF16 = jnp.bfloat16

GRID_W = 64
ROPE_THETA = 10000.0
HEAD64 = 64
GQA_HEADS = 16
GQA_KV = 4
MLA_HEADS = 4
MLA_NOPE = 128
MLA_V = 128
MLA_QR = 512
MLA_KVR = 256
ML_HEADS = 4
ML_D = 128
ML_W = ML_HEADS * ML_D
CHUNK = 128
N_EXPERTS = 8
RMS_EPS = 1e-6
LN_EPS = 1e-5
DEPTH = 2
ALPHA = (2 * DEPTH) ** 0.25

LANES = 128
NEG_BIG = -1e30
LOG2E = 1.4426950408889634
ONES_ROWS = 16
ATTN_BUFS = 4
ATTN_AHEAD = 1
VMEM_LIMIT = 56 * 1024 * 1024
FFN_TM = 1024
GATHER_ROWS_PER_STEP = 64


def _cp(*sem, vmem=VMEM_LIMIT):
    return pltpu.CompilerParams(dimension_semantics=sem, vmem_limit_bytes=vmem)


def _full(shape):
    n = len(shape)
    return pl.BlockSpec(shape, lambda *_: (0,) * n)


def _bf(x):
    return x.astype(BF16)


def _dot(a, b):
    return jnp.dot(a, b, preferred_element_type=F32)


def _split3(x):
    hi = _bf(x)
    r1 = x - hi.astype(F32)
    mid = _bf(r1)
    lo = _bf(r1 - mid.astype(F32))
    return hi, mid, lo


def _ada_kernel(c_ref, w_ref, b_ref, o_ref):
    c = c_ref[...]
    ca = c * jax.nn.sigmoid(c)
    o_ref[...] = jnp.sum(ca * w_ref[...], axis=0, keepdims=True) + b_ref[...]


def _ada(c_col, w, b):
    d, n = w.shape
    tn = 1024
    return pl.pallas_call(
        _ada_kernel,
        out_shape=jax.ShapeDtypeStruct((1, n), F32),
        grid=(n // tn,),
        in_specs=[_full((d, 1)), pl.BlockSpec((d, tn), lambda j: (0, j)), pl.BlockSpec((1, tn), lambda j: (0, j))],
        out_specs=pl.BlockSpec((1, tn), lambda j: (0, j)),
        compiler_params=_cp("parallel"),
        name="ada_mod",
    )(c_col, w, b.reshape(1, n))


def _group_sumsq(blk, bd):
    sq = blk * blk
    hi = _bf(sq)
    lo = _bf(sq - hi.astype(F32))
    return _dot(hi, bd) + _dot(lo, bd)


def _rope(blk, cos, sin_a, sin_b):
    return blk * cos + pltpu.roll(blk, LANES - 16, 1) * sin_a + pltpu.roll(blk, 16, 1) * sin_b


def _modulate(x, scale, shift):
    return x * (1.0 + scale) + shift


def _ones_rows(n):
    r = lax.broadcasted_iota(jnp.int32, (ONES_ROWS, n), 0)
    return jnp.where(r == 0, 1.0, 0.0).astype(BF16)


def _inproj_gqa_kernel(x_ref, sc_ref, sh_ref, w_ref, qg_ref, kg_ref, cos_ref, sa_ref, sb_ref, bd_ref,
                       qT_ref, k_ref, vT_ref):
    u = _bf(_modulate(x_ref[...], sc_ref[...], sh_ref[...]))
    y = _dot(u, w_ref[...])
    cos, sa, sb, bd = cos_ref[...], sa_ref[...], sb_ref[...], bd_ref[...]
    nq = GQA_HEADS * HEAD64 // LANES
    nk = GQA_KV * HEAD64 // LANES
    scale = HEAD64 ** -0.5 * LOG2E
    for j in range(nq):
        blk = y[:, j * LANES:(j + 1) * LANES]
        ss = _group_sumsq(blk, bd)
        blk = blk * lax.rsqrt(ss * (1.0 / HEAD64) + RMS_EPS) * qg_ref[...]
        blk = _rope(blk, cos, sa, sb) * scale
        qT_ref[j * LANES:(j + 1) * LANES, :] = _bf(blk.T)
    for j in range(nk):
        blk = y[:, (nq + j) * LANES:(nq + j + 1) * LANES]
        ss = _group_sumsq(blk, bd)
        blk = blk * lax.rsqrt(ss * (1.0 / HEAD64) + RMS_EPS) * kg_ref[...]
        blk = _bf(_rope(blk, cos, sa, sb))
        k_ref[2 * j] = blk[:, :HEAD64]
        k_ref[2 * j + 1] = blk[:, HEAD64:]
    vT = _bf(y[:, (nq + nk) * LANES:].T)
    ones = _ones_rows(vT.shape[1])
    dve = HEAD64 + ONES_ROWS
    for h in range(GQA_KV):
        vT_ref[h * dve:h * dve + HEAD64, :] = vT[h * HEAD64:(h + 1) * HEAD64, :]
        vT_ref[h * dve + HEAD64:(h + 1) * dve, :] = ones


def _inproj_gqa(x, sc, sh, w, qg, kg, cos, sa, sb, bd, tm=512):
    s, d = x.shape
    n = w.shape[1]
    row = lambda i: (i, 0)
    return pl.pallas_call(
        _inproj_gqa_kernel,
        out_shape=(jax.ShapeDtypeStruct((GQA_HEADS * HEAD64, s), BF16),
                   jax.ShapeDtypeStruct((GQA_KV, s, HEAD64), BF16),
                   jax.ShapeDtypeStruct((GQA_KV * (HEAD64 + ONES_ROWS), s), BF16)),
        grid=(s // tm,),
        in_specs=[pl.BlockSpec((tm, d), row), _full((1, d)), _full((1, d)), _full((d, n)),
                  _full((1, LANES)), _full((1, LANES)),
                  pl.BlockSpec((tm, LANES), row), pl.BlockSpec((tm, LANES), row), pl.BlockSpec((tm, LANES), row),
                  _full((LANES, LANES))],
        out_specs=(pl.BlockSpec((GQA_HEADS * HEAD64, tm), lambda i: (0, i)),
                   pl.BlockSpec((GQA_KV, tm, HEAD64), lambda i: (0, i, 0)),
                   pl.BlockSpec((GQA_KV * (HEAD64 + ONES_ROWS), tm), lambda i: (0, i))),
        compiler_params=_cp("parallel"),
        name="inproj_gqa",
    )(x, sc, sh, w, qg, kg, cos, sa, sb, bd)


def _inproj_mla_kernel(x_ref, sc_ref, sh_ref, w_ref, qn_ref, kvn_ref, wuq_ref, wukv_ref, cos_ref, sa_ref, sb_ref,
                       qT_ref, k_ref, vT_ref):
    u = _bf(_modulate(x_ref[...], sc_ref[...], sh_ref[...]))
    y = _dot(u, w_ref[...])
    cos, sa, sb = cos_ref[...], sa_ref[...], sb_ref[...]
    cq = y[:, :MLA_QR]
    ckv = y[:, MLA_QR:MLA_QR + MLA_KVR]
    kr = y[:, MLA_QR + MLA_KVR:]
    cqn = cq * lax.rsqrt(jnp.mean(cq * cq, axis=-1, keepdims=True) + RMS_EPS) * qn_ref[...]
    ckvn = ckv * lax.rsqrt(jnp.mean(ckv * ckv, axis=-1, keepdims=True) + RMS_EPS) * kvn_ref[...]
    qm = _dot(_bf(cqn), wuq_ref[...])
    kvm = _dot(_bf(ckvn), wukv_ref[...])
    scale = (MLA_NOPE + HEAD64) ** -0.5 * LOG2E
    dq = MLA_NOPE + HEAD64
    dve = MLA_V + ONES_ROWS
    ones = _ones_rows(y.shape[0])
    k_rope = _bf(_rope(kr, cos, sa, sb)[:, :HEAD64])
    for p in range(MLA_HEADS // 2):
        rblk = qm[:, MLA_HEADS * MLA_NOPE + p * LANES:MLA_HEADS * MLA_NOPE + (p + 1) * LANES]
        rT = _bf((_rope(rblk, cos, sa, sb) * scale).T)
        for e in range(2):
            h = 2 * p + e
            qT_ref[h * dq + MLA_NOPE:(h + 1) * dq, :] = rT[e * HEAD64:(e + 1) * HEAD64, :]
    for h in range(MLA_HEADS):
        nope = qm[:, h * MLA_NOPE:(h + 1) * MLA_NOPE] * scale
        qT_ref[h * dq:h * dq + MLA_NOPE, :] = _bf(nope.T)
        k_ref[h, :, :MLA_NOPE] = _bf(kvm[:, h * 2 * LANES:h * 2 * LANES + MLA_NOPE])
        k_ref[h, :, MLA_NOPE:] = k_rope
        vT_ref[h * dve:h * dve + MLA_V, :] = _bf(kvm[:, h * 2 * LANES + MLA_NOPE:(h + 1) * 2 * LANES].T)
        vT_ref[h * dve + MLA_V:(h + 1) * dve, :] = ones


def _inproj_mla(x, sc, sh, w, qn, kvn, wuq, wukv, cos, sa, sb, tm=512):
    s, d = x.shape
    n = w.shape[1]
    dq = MLA_NOPE + HEAD64
    row = lambda i: (i, 0)
    return pl.pallas_call(
        _inproj_mla_kernel,
        out_shape=(jax.ShapeDtypeStruct((MLA_HEADS * dq, s), BF16),
                   jax.ShapeDtypeStruct((MLA_HEADS, s, dq), BF16),
                   jax.ShapeDtypeStruct((MLA_HEADS * (MLA_V + ONES_ROWS), s), BF16)),
        grid=(s // tm,),
        in_specs=[pl.BlockSpec((tm, d), row), _full((1, d)), _full((1, d)), _full((d, n)),
                  _full((1, MLA_QR)), _full((1, MLA_KVR)), _full(wuq.shape), _full(wukv.shape),
                  pl.BlockSpec((tm, LANES), row), pl.BlockSpec((tm, LANES), row), pl.BlockSpec((tm, LANES), row)],
        out_specs=(pl.BlockSpec((MLA_HEADS * dq, tm), lambda i: (0, i)),
                   pl.BlockSpec((MLA_HEADS, tm, dq), lambda i: (0, i, 0)),
                   pl.BlockSpec((MLA_HEADS * (MLA_V + ONES_ROWS), tm), lambda i: (0, i))),
        compiler_params=_cp("parallel"),
        name="inproj_mla",
    )(x, sc, sh, w, qn, kvn, wuq, wukv, cos, sa, sb)


def _inproj_mlstm_kernel(x_ref, xp_ref, xn_ref, sc_ref, sh_ref, w_ref, conv_ref, gb_ref,
                         q_ref, kT_ref, v_ref, o_ref, gT_ref):
    i = pl.program_id(0)
    last = pl.num_programs(0) - 1
    sc, sh = sc_ref[...], sh_ref[...]
    u = _bf(_modulate(x_ref[...], sc, sh))
    y = _dot(u, w_ref[...])
    tm = y.shape[0]
    qk = y[:, :2 * ML_W]
    halo = jnp.concatenate([xp_ref[...], xn_ref[...]], axis=0)
    hy = _dot(_bf(_modulate(halo, sc, sh)), w_ref[:, :2 * ML_W])
    prev_row = jnp.where(i == 0, 0.0, hy[7:8, :])
    next_row = jnp.where(i == last, 0.0, hy[8:9, :])
    rows = lax.broadcasted_iota(jnp.int32, qk.shape, 0)
    prev = jnp.where(rows == 0, prev_row, pltpu.roll(qk, 1, 0))
    nxt = jnp.where(rows == tm - 1, next_row, pltpu.roll(qk, tm - 1, 0))
    cw = conv_ref[...]
    z = cw[0:1, :] * prev + cw[1:2, :] * qk + cw[2:3, :] * nxt
    z = z * jax.nn.sigmoid(z)
    q_ref[...] = _bf(z[:, :ML_W])
    kT_ref[...] = (z[:, ML_W:] * (ML_D ** -0.5)).T
    v_ref[...] = _bf(y[:, 2 * ML_W:3 * ML_W])
    o_ref[...] = y[:, 3 * ML_W:4 * ML_W]
    g = y[:, 4 * ML_W:] + gb_ref[...]
    gT_ref[...] = g.T[:4 * ML_HEADS, :]


def _inproj_mlstm(x, sc, sh, w, conv, gb, tm=512):
    s, d = x.shape
    n = w.shape[1]
    row = lambda i: (i, 0)
    col = lambda i: (0, i)
    r8 = tm // 8
    return pl.pallas_call(
        _inproj_mlstm_kernel,
        out_shape=(jax.ShapeDtypeStruct((s, ML_W), BF16), jax.ShapeDtypeStruct((ML_W, s), F32),
                   jax.ShapeDtypeStruct((s, ML_W), BF16), jax.ShapeDtypeStruct((s, ML_W), F32),
                   jax.ShapeDtypeStruct((4 * ML_HEADS, s), F32)),
        grid=(s // tm,),
        in_specs=[pl.BlockSpec((tm, d), row),
                  pl.BlockSpec((8, d), lambda i: (jnp.maximum(i * r8 - 1, 0), 0)),
                  pl.BlockSpec((8, d), lambda i: (jnp.minimum((i + 1) * r8, s // 8 - 1), 0)),
                  _full((1, d)), _full((1, d)), _full((d, n)), _full((3, 2 * ML_W)), _full((1, LANES))],
        out_specs=(pl.BlockSpec((tm, ML_W), row), pl.BlockSpec((ML_W, tm), col),
                   pl.BlockSpec((tm, ML_W), row), pl.BlockSpec((tm, ML_W), row),
                   pl.BlockSpec((4 * ML_HEADS, tm), col)),
        compiler_params=_cp("parallel"),
        name="inproj_mlstm",
    )(x, x, x, sc, sh, w, conv, gb)


def _attn_kernel(qT_ref, k_ref, vT_ref, o_ref, s_sc, acc_sc, *, groups, dq, dv, tk):
    hp = k_ref.shape[0]
    seq = k_ref.shape[1]
    n = seq // tk
    dve = dv + ONES_ROWS
    cq = qT_ref.shape[1]
    nq = groups * cq
    qTs = []
    for hh in range(hp):
        r0 = hh * groups * dq
        qTs.append(jnp.concatenate([qT_ref[r0 + g * dq:r0 + (g + 1) * dq, :] for g in range(groups)], axis=1)
                   if groups > 1 else qT_ref[r0:r0 + dq, :])
    acc_sc[...] = jnp.zeros(acc_sc.shape, F32)

    def scores(hh, c, buf):
        off = c * tk if isinstance(c, int) else pl.multiple_of(c * tk, tk)
        s = _dot(k_ref[hh, pl.ds(off, tk), :], qTs[hh])
        s_sc[hh, buf] = s
        return jnp.max(s, axis=0, keepdims=True)

    def accumulate(hh, c, buf, mx, m_old):
        off = c * tk if isinstance(c, int) else pl.multiple_of(c * tk, tk)
        m_new = jnp.maximum(m_old, mx)
        p = jnp.exp2(_bf(s_sc[hh, buf] - m_new))
        pv = _dot(vT_ref[hh * dve:(hh + 1) * dve, pl.ds(off, tk)], p)
        acc_sc[hh] = jnp.exp2(m_old - m_new) * acc_sc[hh] + pv
        return m_new

    def body(base, carry, tail=False):
        mxs, ms = [list(t) for t in carry[0]], list(carry[1])
        for j in range(ATTN_BUFS):
            for hh in range(hp):
                if not (tail and j + ATTN_AHEAD >= ATTN_BUFS):
                    mxs[hh].append(scores(hh, base + j + ATTN_AHEAD, (j + ATTN_AHEAD) % ATTN_BUFS))
                ms[hh] = accumulate(hh, base + j, j, mxs[hh].pop(0), ms[hh])
        return tuple(tuple(t) for t in mxs), tuple(ms)

    carry = (tuple(tuple(scores(hh, c, c) for c in range(ATTN_AHEAD)) for hh in range(hp)),
             tuple(jnp.full((1, nq), NEG_BIG, F32) for _ in range(hp)))
    carry = lax.fori_loop(0, n // ATTN_BUFS - 1, lambda i, cr: body(i * ATTN_BUFS, cr), carry)
    body(n - ATTN_BUFS, carry, tail=True)
    for hh in range(hp):
        oT = acc_sc[hh, :dv, :] * (1.0 / acc_sc[hh, dv:dv + 1, :])
        if groups > 1:
            oT = jnp.concatenate([oT[:, g * cq:(g + 1) * cq] for g in range(groups)], axis=0)
        o_ref[:, hh * groups * dv:(hh + 1) * groups * dv] = _bf(oT.T)


def _attention(qT, k, vT, *, groups, dq, dv, cq, hp, tk=512):
    h, s, _ = k.shape
    nq = groups * cq
    dve = dv + ONES_ROWS
    assert (s // tk) % ATTN_BUFS == 0 and h % hp == 0
    kern = functools.partial(_attn_kernel, groups=groups, dq=dq, dv=dv, tk=tk)
    return pl.pallas_call(
        kern,
        out_shape=jax.ShapeDtypeStruct((s, h * groups * dv), BF16),
        grid=(h // hp, s // cq),
        in_specs=[pl.BlockSpec((hp * groups * dq, cq), lambda a, i: (a, i)),
                  pl.BlockSpec((hp, s, dq), lambda a, i: (a, 0, 0)),
                  pl.BlockSpec((hp * dve, s), lambda a, i: (a, 0))],
        out_specs=pl.BlockSpec((cq, hp * groups * dv), lambda a, i: (i, a)),
        scratch_shapes=[pltpu.VMEM((hp, ATTN_BUFS, tk, nq), F32), pltpu.VMEM((hp, dve, nq), F32)],
        compiler_params=_cp("parallel", "parallel"),
        name="attention_g%d" % groups,
    )(qT, k, vT)


def _mlstm_kernel(q_ref, kT_ref, v_ref, o_ref, g_ref, hn_ref, out_ref,
                  cn_sc, mprev_sc, w_sc, r_sc, lf_sc, g_sc, mloc_sc, hacc_sc):
    head = pl.program_id(0)
    nc = g_ref.shape[1]
    L = CHUNK
    ii = lax.broadcasted_iota(jnp.int32, (L, L), 0)
    jj = lax.broadcasted_iota(jnp.int32, (L, L), 1)
    e0 = jnp.where(lax.broadcasted_iota(jnp.int32, (L, LANES), 1) == 0, 1.0, 0.0).astype(BF16)

    for direction in range(2):
        fw = direction == 0
        mask = (jj <= ii) if fw else (jj >= ii)
        tri = jnp.where((ii <= jj) if fw else (ii >= jj), 1.0, 0.0).astype(BF16)
        li = g_ref[2 * ML_HEADS * direction + head]
        fg = g_ref[2 * ML_HEADS * direction + ML_HEADS + head]
        lf = jnp.minimum(fg, 0.0) - jnp.log1p(jnp.exp(-jnp.abs(fg)))
        hi, mid, lo = _split3(lf)
        b = _dot(hi, tri) + _dot(mid, tri) + _dot(lo, tri)
        gtot = jnp.sum(lf, axis=-1, keepdims=True)
        a = gtot - b + li
        mloc = jnp.max(a, axis=-1, keepdims=True)
        w_sc[...] = jnp.exp(a - mloc)
        r_sc[...] = li - b
        lf_sc[...] = lf
        g_sc[...] = jnp.broadcast_to(gtot, (nc, LANES))
        mloc_sc[...] = jnp.broadcast_to(mloc, (nc, LANES))

        def scan_body(i, carry, fw=fw):
            cn, m = carry
            cc = i if fw else nc - 1 - i
            off = pl.multiple_of(cc * L, L)
            cn_sc[cc] = _bf(cn)
            mprev_sc[pl.ds(cc, 1), :] = jnp.broadcast_to(m, (1, LANES))
            kw = _bf(kT_ref[:, pl.ds(off, L)] * w_sc[pl.ds(cc, 1), :])
            vext = jnp.concatenate([v_ref[pl.ds(off, L), :], e0], axis=1)
            kv = _dot(kw, vext)
            g = g_sc[pl.ds(cc, 1), 0:1]
            ml = mloc_sc[pl.ds(cc, 1), 0:1]
            m_new = jnp.maximum(g + m, ml)
            cn = jnp.exp(g + m - m_new) * cn + jnp.exp(ml - m_new) * kv
            return cn, m_new

        lax.fori_loop(0, nc, scan_body, (jnp.zeros((ML_D, 2 * ML_D), F32), jnp.zeros((1, 1), F32)), unroll=4)

        def out_body(cc, carry, fw=fw, mask=mask):
            off = pl.multiple_of(cc * L, L)
            mp = mprev_sc[pl.ds(cc, 1), 0:1]
            rm = jnp.where(mask, r_sc[pl.ds(cc, 1), :], -jnp.inf)
            mt = jnp.maximum(mp, jnp.max(rm, axis=-1, keepdims=True))
            d = jnp.exp(rm - mt)
            iw = jnp.exp(mp - mt)
            qc = q_ref[pl.ds(off, L), :]
            qk = _dot(qc, _bf(kT_ref[:, pl.ds(off, L)])) * d
            num_i = _dot(_bf(qk), v_ref[pl.ds(off, L), :])
            den_i = jnp.sum(qk, axis=-1, keepdims=True)
            inter = _dot(qc, cn_sc[cc])
            num = num_i + iw * inter[:, :ML_D]
            den = den_i + iw * inter[:, ML_D:ML_D + 1]
            bcol = jnp.sum(jnp.where(mask, lf_sc[pl.ds(cc, 1), :], 0.0), axis=-1, keepdims=True)
            h = num / jnp.maximum(jnp.abs(den), jnp.exp(-(bcol + mt)))
            if fw:
                hacc_sc[pl.ds(off, L), :] = h
            else:
                h = h + hacc_sc[pl.ds(off, L), :]
                mu = jnp.mean(h, axis=-1, keepdims=True)
                hc = h - mu
                var = jnp.mean(hc * hc, axis=-1, keepdims=True)
                hn = hc * lax.rsqrt(var + LN_EPS) * hn_ref[...]
                out_ref[pl.ds(off, L), :] = _bf(jax.nn.sigmoid(o_ref[pl.ds(off, L), :]) * hn)
            return carry

        lax.fori_loop(0, nc, out_body, 0, unroll=4)


def _mlstm(q, kT, v, o, g3, hn):
    s = q.shape[0]
    nc = s // CHUNK
    colblk = lambda h: (0, h)
    return pl.pallas_call(
        _mlstm_kernel,
        out_shape=jax.ShapeDtypeStruct((s, ML_W), BF16),
        grid=(ML_HEADS,),
        in_specs=[pl.BlockSpec((s, ML_D), colblk), pl.BlockSpec((ML_D, s), lambda h: (h, 0)),
                  pl.BlockSpec((s, ML_D), colblk), pl.BlockSpec((s, ML_D), colblk),
                  _full(g3.shape), pl.BlockSpec((1, ML_D), colblk)],
        out_specs=pl.BlockSpec((s, ML_D), colblk),
        scratch_shapes=[pltpu.VMEM((nc, ML_D, 2 * ML_D), BF16), pltpu.VMEM((nc, LANES), F32),
                        pltpu.VMEM((nc, CHUNK), F32), pltpu.VMEM((nc, CHUNK), F32), pltpu.VMEM((nc, CHUNK), F32),
                        pltpu.VMEM((nc, LANES), F32), pltpu.VMEM((nc, LANES), F32),
                        pltpu.VMEM((s, ML_D), F32)],
        compiler_params=_cp("parallel"),
        name="mlstm",
    )(q, kT, v, o, g3, hn)


def _layer_norm(z, g, b):
    mu = jnp.mean(z, axis=-1, keepdims=True)
    zc = z - mu
    var = jnp.mean(zc * zc, axis=-1, keepdims=True)
    return zc * lax.rsqrt(var + LN_EPS) * g + b


def _outproj_kernel(*refs, route):
    if route:
        (x_ref, a_ref, b_ref, c_ref, w_ref, gate_ref, g_ref, bb_ref, sc_ref, sh_ref, rw_both_ref, rw_high_ref,
         x1_ref, u2_ref, lg_ref) = refs
    else:
        x_ref, a_ref, b_ref, c_ref, w_ref, gate_ref, g_ref, bb_ref, sc_ref, sh_ref, x1_ref, u2_ref = refs
    na, nb = a_ref.shape[1], b_ref.shape[1]
    y = _dot(a_ref[...], w_ref[:na, :]) + _dot(b_ref[...], w_ref[na:na + nb, :]) + _dot(c_ref[...], w_ref[na + nb:, :])
    x1 = _layer_norm(ALPHA * x_ref[...] + gate_ref[...] * y, g_ref[...], bb_ref[...])
    x1_ref[...] = x1
    u2 = _modulate(x1, sc_ref[...], sh_ref[...])
    u2_ref[...] = u2.astype(u2_ref.dtype)
    if route:
        hi = _bf(u2)
        lo = _bf(u2 - hi.astype(F32))
        both = _dot(hi, rw_both_ref[...])
        lg_ref[...] = both[:, :LANES] + (_dot(lo, rw_high_ref[...]) + both[:, LANES:])


def _outproj(x, a, b, c, w, gate, g, bb, sc, sh, rw=None, tm=512):
    s, d = x.shape
    row = lambda i: (i, 0)
    route = rw is not None
    ins = [x, a, b, c, w, gate, g, bb, sc, sh]
    in_specs = [pl.BlockSpec((tm, d), row), pl.BlockSpec((tm, a.shape[1]), row), pl.BlockSpec((tm, b.shape[1]), row),
                pl.BlockSpec((tm, c.shape[1]), row), _full(w.shape)] + [_full((1, d))] * 5
    out_shape = [jax.ShapeDtypeStruct((s, d), F32), jax.ShapeDtypeStruct((s, d), F32 if route else BF16)]
    out_specs = [pl.BlockSpec((tm, d), row), pl.BlockSpec((tm, d), row)]
    if route:
        rwp = jnp.zeros((d, LANES), F32).at[:, :rw.shape[1]].set(rw)
        rwh = _bf(rwp)
        rwl = _bf(rwp - rwh.astype(F32))
        ins += [jnp.concatenate([rwh, rwl], axis=1), rwh]
        in_specs += [_full((d, 2 * LANES)), _full((d, LANES))]
        out_shape.append(jax.ShapeDtypeStruct((s, LANES), F32))
        out_specs.append(pl.BlockSpec((tm, LANES), row))
    return pl.pallas_call(
        functools.partial(_outproj_kernel, route=route),
        out_shape=tuple(out_shape),
        grid=(s // tm,),
        in_specs=in_specs,
        out_specs=tuple(out_specs),
        compiler_params=_cp("parallel"),
        name="outproj_ln1",
    )(*ins)


def _swiglu_partial(x, wg, wu, wd):
    gt = _dot(x, _bf(wg))
    up = _dot(x, _bf(wu))
    return _dot(_bf(gt * jax.nn.sigmoid(gt) * up), _bf(wd))


def _ffn_dense_kernel(x_ref, wg_ref, wu_ref, wd_ref, y_ref):
    j = pl.program_id(1)

    @pl.when(j == 0)
    def _():
        y_ref[...] = _swiglu_partial(x_ref[...], wg_ref[...], wu_ref[...], wd_ref[...])

    @pl.when(j > 0)
    def _():
        y_ref[...] += _swiglu_partial(x_ref[...], wg_ref[...], wu_ref[...], wd_ref[...])


def _ffn_dense(x, wg, wu, wd, tm=FFN_TM, tf=256):
    s, d = x.shape
    f = wg.shape[1]
    return pl.pallas_call(
        _ffn_dense_kernel,
        out_shape=jax.ShapeDtypeStruct((s, d), F32),
        grid=(s // tm, f // tf),
        in_specs=[pl.BlockSpec((tm, d), lambda i, j: (i, 0)), pl.BlockSpec((d, tf), lambda i, j: (0, j)),
                  pl.BlockSpec((d, tf), lambda i, j: (0, j)), pl.BlockSpec((tf, d), lambda i, j: (j, 0))],
        out_specs=pl.BlockSpec((tm, d), lambda i, j: (i, 0)),
        compiler_params=_cp("parallel", "arbitrary"),
        name="swiglu_dense",
    )(x, wg, wu, wd)


def _ffn_routed_kernel(te_ref, nu_ref, src_ref, u_hbm, wg_ref, wu_ref, wd_ref, y_ref, xf_buf, xb_buf, sem):
    i, j = pl.program_id(0), pl.program_id(1)
    n_live = nu_ref[0]
    live = i < n_live
    tm = y_ref.shape[0]

    def row_copy(tile, r):
        return pltpu.make_async_copy(u_hbm.at[pl.ds(src_ref[tile * tm + r], 1)], xf_buf.at[pl.ds(r, 1)], sem)

    def issue(tile, first, count):
        def body(r, carry):
            row_copy(tile, first + r).start(priority=1)
            return carry
        lax.fori_loop(0, count, body, 0, unroll=8)

    @pl.when((i == 0) & (j == 0))
    def _():
        issue(0, 0, tm)

    @pl.when(live & (j == 0))
    def _():
        pltpu.make_async_copy(u_hbm.at[pl.ds(0, tm)], xf_buf, sem).wait()
        xb_buf[...] = _bf(xf_buf[...])
        y_ref[...] = _swiglu_partial(xb_buf[...], wg_ref[0], wu_ref[0], wd_ref[0])

    @pl.when((j >= 1) & (j <= tm // GATHER_ROWS_PER_STEP) & (i + 1 < n_live))
    def _():
        issue(i + 1, (j - 1) * GATHER_ROWS_PER_STEP, GATHER_ROWS_PER_STEP)

    @pl.when(live & (j > 0))
    def _():
        y_ref[...] += _swiglu_partial(xb_buf[...], wg_ref[0], wu_ref[0], wd_ref[0])

    @pl.when(jnp.logical_not(live) & (j == 0))
    def _():
        y_ref[...] = jnp.zeros(y_ref.shape, F32)


def _ffn_routed(u, src, wg, wu, wd, tile_expert, n_used, tm, tf=256):
    s, d = u.shape
    f = wg.shape[2]
    nt, nf = src.shape[0] // tm, f // tf
    assert tm % GATHER_ROWS_PER_STEP == 0 and nf > tm // GATHER_ROWS_PER_STEP

    def clamp(i, j, nu):
        live = i < nu[0]
        return jnp.where(live, i, nu[0] - 1), jnp.where(live, j, nf - 1)

    def gmap(i, j, te, nu, sr):
        ii, jj = clamp(i, j, nu)
        return te[ii], 0, jj

    def dmap(i, j, te, nu, sr):
        ii, jj = clamp(i, j, nu)
        return te[ii], jj, 0

    return pl.pallas_call(
        _ffn_routed_kernel,
        out_shape=jax.ShapeDtypeStruct((nt * tm, d), F32),
        grid_spec=pltpu.PrefetchScalarGridSpec(
            num_scalar_prefetch=3,
            grid=(nt, nf),
            in_specs=[pl.BlockSpec(memory_space=pl.ANY), pl.BlockSpec((1, d, tf), gmap), pl.BlockSpec((1, d, tf), gmap),
                      pl.BlockSpec((1, tf, d), dmap)],
            out_specs=pl.BlockSpec((tm, d), lambda i, j, te, nu, sr: (i, 0)),
            scratch_shapes=[pltpu.VMEM((tm, d), F32), pltpu.VMEM((tm, d), BF16), pltpu.SemaphoreType.DMA(())]),
        compiler_params=_cp("arbitrary", "arbitrary"),
        name="swiglu_routed",
    )(tile_expert, n_used, src, u, wg, wu, wd)


def _ln2_kernel(x_ref, f_ref, gate_ref, g_ref, b_ref, o_ref):
    o_ref[...] = _layer_norm(ALPHA * x_ref[...] + gate_ref[...] * f_ref[...], g_ref[...], b_ref[...])


def _ln2(x, f, gate, g, b, tm=512):
    s, d = x.shape
    row = lambda i: (i, 0)
    return pl.pallas_call(
        _ln2_kernel,
        out_shape=jax.ShapeDtypeStruct((s, d), F32),
        grid=(s // tm,),
        in_specs=[pl.BlockSpec((tm, d), row), pl.BlockSpec((tm, d), row)] + [_full((1, d))] * 3,
        out_specs=pl.BlockSpec((tm, d), row),
        compiler_params=_cp("parallel"),
        name="residual_ln2",
    )(x, f, gate, g, b)


def _row_copy(y_hbm, idx_ref, buf, sem, tile, slot, r, tm):
    return pltpu.make_async_copy(y_hbm.at[pl.ds(idx_ref[tile * tm + r], 1)], buf.at[slot, pl.ds(r, 1)], sem.at[slot])


def _ln2_combine_kernel(d1_ref, d2_ref, x_ref, info_ref, y_hbm, gate_ref, g_ref, b_ref, o_ref, a_buf, b_buf, sem):
    i = pl.program_id(0)
    n = pl.num_programs(0)
    tm = x_ref.shape[0]

    def issue(tile, slot):
        def body(r, carry):
            _row_copy(y_hbm, d1_ref, a_buf, sem, tile, slot, r, tm).start()
            _row_copy(y_hbm, d2_ref, b_buf, sem, tile, slot, r, tm).start()
            return carry
        lax.fori_loop(0, tm, body, 0, unroll=8)

    @pl.when(i == 0)
    def _():
        issue(0, 0)

    @pl.when(i + 1 < n)
    def _():
        issue(i + 1, (i + 1) % 2)

    slot = i % 2
    pltpu.make_async_copy(y_hbm.at[pl.ds(0, tm)], a_buf.at[slot], sem.at[slot]).wait()
    pltpu.make_async_copy(y_hbm.at[pl.ds(0, tm)], b_buf.at[slot], sem.at[slot]).wait()
    info = info_ref[...]
    f = info[:, 2:3] * a_buf[slot] + info[:, 3:4] * b_buf[slot]
    o_ref[...] = _layer_norm(ALPHA * x_ref[...] + gate_ref[...] * f, g_ref[...], b_ref[...])


def _ln2_combine(x, info, y, d1, d2, gate, g, b, tm=256):
    s, d = x.shape
    row = lambda i, *_: (i, 0)
    one = lambda i, *_: (0, 0)
    return pl.pallas_call(
        _ln2_combine_kernel,
        out_shape=jax.ShapeDtypeStruct((s, d), F32),
        grid_spec=pltpu.PrefetchScalarGridSpec(
            num_scalar_prefetch=2,
            grid=(s // tm,),
            in_specs=[pl.BlockSpec((tm, d), row), pl.BlockSpec((tm, LANES), row), pl.BlockSpec(memory_space=pl.ANY),
                      pl.BlockSpec((1, d), one), pl.BlockSpec((1, d), one), pl.BlockSpec((1, d), one)],
            out_specs=pl.BlockSpec((tm, d), row),
            scratch_shapes=[pltpu.VMEM((2, tm, d), F32), pltpu.VMEM((2, tm, d), F32), pltpu.SemaphoreType.DMA((2,))]),
        compiler_params=_cp("arbitrary"),
        name="combine_ln2",
    )(d1, d2, x, info, y, gate, g, b)


def _router_kernel(lg_ref, info_ref, cnt_ref, carry_sc):
    i = pl.program_id(0)

    @pl.when(i == 0)
    def _():
        carry_sc[...] = jnp.zeros(carry_sc.shape, F32)

    lg = lg_ref[...]
    tm = lg.shape[0]
    lane = lax.broadcasted_iota(jnp.int32, lg.shape, 1)
    lg = jnp.where(lane < N_EXPERTS, lg, -jnp.inf)
    v1 = jnp.max(lg, axis=-1, keepdims=True)
    i1 = jnp.min(jnp.where(lg == v1, lane, LANES), axis=-1, keepdims=True)
    lg2 = jnp.where(lane == i1, -jnp.inf, lg)
    v2 = jnp.max(lg2, axis=-1, keepdims=True)
    i2 = jnp.min(jnp.where(lg2 == v2, lane, LANES), axis=-1, keepdims=True)
    e21 = jnp.exp(v2 - v1)
    w1 = 1.0 / (1.0 + e21)
    w2 = e21 * w1
    oh1 = lane == i1
    oh2 = lane == i2
    oh = jnp.where(oh1 | oh2, 1.0, 0.0)
    rr = lax.broadcasted_iota(jnp.int32, (tm, tm), 0)
    cc = lax.broadcasted_iota(jnp.int32, (tm, tm), 1)
    strict = jnp.where(cc < rr, 1.0, 0.0).astype(BF16)
    before = _dot(strict, _bf(oh)) + carry_sc[...]
    r1 = jnp.sum(jnp.where(oh1, before, 0.0), axis=-1, keepdims=True)
    r2 = jnp.sum(jnp.where(oh2, before, 0.0), axis=-1, keepdims=True)
    vals = (i1.astype(F32), i2.astype(F32), w1, w2, r1, r2)
    info = jnp.zeros(lg.shape, F32)
    for n, v in enumerate(vals):
        info = jnp.where(lane == n, v, info)
    info_ref[...] = info
    carry_sc[...] += jnp.sum(oh, axis=0, keepdims=True)
    cnt_ref[...] = carry_sc[...]


def _router(logits, tm=512):
    s = logits.shape[0]
    return pl.pallas_call(
        _router_kernel,
        out_shape=(jax.ShapeDtypeStruct((s, LANES), F32), jax.ShapeDtypeStruct((1, LANES), F32)),
        grid=(s // tm,),
        in_specs=[pl.BlockSpec((tm, LANES), lambda i: (i, 0))],
        out_specs=(pl.BlockSpec((tm, LANES), lambda i: (i, 0)), _full((1, LANES))),
        scratch_shapes=[pltpu.VMEM((1, LANES), F32)],
        compiler_params=_cp("arbitrary"),
        name="router_top2",
    )(logits)


def _rope_tables(s):
    rows = s // GRID_W
    n_freq = HEAD64 // 4
    inv_freq = ROPE_THETA ** (-jnp.arange(n_freq, dtype=F32) / n_freq)
    lane = jnp.arange(LANES)
    freq = inv_freq[lane % n_freq]
    ang_r = jnp.arange(rows, dtype=F32)[:, None] * freq
    ang_c = jnp.arange(GRID_W, dtype=F32)[:, None] * freq
    by_row = (lane % HEAD64) < HEAD64 // 2

    def table(fn):
        full = jnp.where(by_row, fn(ang_r)[:, None, :], fn(ang_c)[None, :, :])
        return full.reshape(s, LANES)

    cos, sin = table(jnp.cos), table(jnp.sin)
    first_half = (jnp.arange(LANES) % 32) < 16
    return cos, jnp.where(first_half, -sin, 0.0), jnp.where(first_half, 0.0, sin)


def _moe_ffn(u2, logits, wg, wu, wd, tm):
    s, d = u2.shape
    e = wg.shape[0]
    info, counts = _router(logits)
    e1, e2 = info[:, 0].astype(jnp.int32), info[:, 1].astype(jnp.int32)
    w1, w2 = info[:, 2], info[:, 3]
    r1, r2 = info[:, 4].astype(jnp.int32), info[:, 5].astype(jnp.int32)
    cnt = counts[0, :e].astype(jnp.int32)
    tiles = (cnt + tm - 1) // tm
    tile_end = jnp.cumsum(tiles)
    start = (tile_end - tiles) * tm
    nt = (2 * s) // tm + e
    n_used = tile_end[-1:].astype(jnp.int32)
    tile_expert = jnp.sum((jnp.arange(nt)[:, None] >= tile_end[None, :]).astype(jnp.int32), axis=1)
    tile_expert = jnp.minimum(tile_expert, e - 1)
    d1, d2 = start[e1] + r1, start[e2] + r2
    tok = jnp.arange(s, dtype=jnp.int32)
    src = jnp.zeros((nt * tm,), jnp.int32).at[jnp.concatenate([d1, d2])].set(jnp.concatenate([tok, tok]))
    y = _ffn_routed(u2, src, wg, wu, wd, tile_expert, n_used, tm=tm)
    return info, y, d1, d2


def _layer(x, c_col, tabs, p, moe):
    s, d = x.shape
    cos, sa, sb = tabs
    mod = _ada(c_col, p["ada_w"], p["ada_b"])
    sh1, sc1, g1, sh2, sc2, g2 = [mod[:, i * d:(i + 1) * d] for i in range(6)]

    w_in = p["w_in"]
    n_gqa = (GQA_HEADS + 2 * GQA_KV) * HEAD64
    n_mla = MLA_QR + MLA_KVR + HEAD64
    w_gqa = _bf(w_in[:, :n_gqa])
    w_mla = _bf(jnp.pad(w_in[:, n_gqa:n_gqa + n_mla], ((0, 0), (0, LANES - HEAD64))))
    w_ml = _bf(jnp.pad(w_in[:, n_gqa + n_mla:], ((0, 0), (0, LANES - 4 * ML_HEADS))))

    ii = jnp.arange(LANES)
    bd = _bf((ii[:, None] // HEAD64 == ii[None, :] // HEAD64).astype(F32))
    qg = jnp.tile(p["gqa_q_norm"], 2).reshape(1, LANES)
    kg = jnp.tile(p["gqa_k_norm"], 2).reshape(1, LANES)
    qT, k, vT = _inproj_gqa(x, sc1, sh1, w_gqa, qg, kg, cos, sa, sb, bd)
    out_a = _attention(qT, k, vT, groups=GQA_HEADS // GQA_KV, dq=HEAD64, dv=HEAD64, cq=128, hp=4)

    dq = MLA_NOPE + HEAD64
    wuq = p["mla_w_uq"].reshape(MLA_QR, MLA_HEADS, dq)
    wuq = _bf(jnp.concatenate([wuq[:, :, :MLA_NOPE].reshape(MLA_QR, -1), wuq[:, :, MLA_NOPE:].reshape(MLA_QR, -1)], axis=1))
    qTm, km, vTm = _inproj_mla(x, sc1, sh1, w_mla, p["mla_q_norm"].reshape(1, -1), p["mla_kv_norm"].reshape(1, -1),
                               wuq, _bf(p["mla_w_ukv"]), cos, sa, sb)
    out_b = _attention(qTm, km, vTm, groups=1, dq=dq, dv=MLA_V, cq=512, hp=2)

    gb = jnp.pad(p["mlstm_gate_b"], (0, LANES - 4 * ML_HEADS)).reshape(1, LANES)
    mq, mkT, mv, mo, gT = _inproj_mlstm(x, sc1, sh1, w_ml, p["mlstm_conv"], gb)
    out_c = _mlstm(mq, mkT, mv, mo, gT.reshape(4 * ML_HEADS, s // CHUNK, CHUNK), p["mlstm_head_norm"].reshape(1, -1))

    ln = [p[n].reshape(1, d) for n in ("ln1_g", "ln1_b", "ln2_g", "ln2_b")]
    if moe:
        x1, u2, logits = _outproj(x, out_a, out_b, out_c, _bf(p["w_out"]), g1, ln[0], ln[1], sc2, sh2, rw=p["router_w"])
        info, y, d1, d2 = _moe_ffn(u2, logits, p["moe_w_gate"], p["moe_w_up"], p["moe_w_down"], tm=FFN_TM)
        return _ln2_combine(x1, info, y, d1, d2, g2, ln[2], ln[3])
    else:
        x1, u2 = _outproj(x, out_a, out_b, out_c, _bf(p["w_out"]), g1, ln[0], ln[1], sc2, sh2)
        f = _ffn_dense(u2, p["ffn_w_gate"], p["ffn_w_up"], p["ffn_w_down"])
    return _ln2(x1, f, g2, ln[2], ln[3])


_NAMES = ("ada_w", "ada_b", "w_in", "gqa_q_norm", "gqa_k_norm", "mla_q_norm", "mla_w_uq", "mla_kv_norm", "mla_w_ukv",
          "mlstm_conv", "mlstm_gate_b", "mlstm_head_norm", "w_out", "ln1_g", "ln1_b")


def kernel(x, c, l0_ada_w, l0_ada_b, l0_w_in, l0_gqa_q_norm, l0_gqa_k_norm, l0_mla_q_norm, l0_mla_w_uq, l0_mla_kv_norm, l0_mla_w_ukv, l0_mlstm_conv, l0_mlstm_gate_b, l0_mlstm_head_norm, l0_w_out, l0_ln1_g, l0_ln1_b, l0_ffn_w_gate, l0_ffn_w_up, l0_ffn_w_down, l0_ln2_g, l0_ln2_b, l1_ada_w, l1_ada_b, l1_w_in, l1_gqa_q_norm, l1_gqa_k_norm, l1_mla_q_norm, l1_mla_w_uq, l1_mla_kv_norm, l1_mla_w_ukv, l1_mlstm_conv, l1_mlstm_gate_b, l1_mlstm_head_norm, l1_w_out, l1_ln1_g, l1_ln1_b, l1_router_w, l1_moe_w_gate, l1_moe_w_up, l1_moe_w_down, l1_ln2_g, l1_ln2_b):
    b, s, d = x.shape
    p0 = dict(zip(_NAMES, (l0_ada_w, l0_ada_b, l0_w_in, l0_gqa_q_norm, l0_gqa_k_norm, l0_mla_q_norm, l0_mla_w_uq,
                           l0_mla_kv_norm, l0_mla_w_ukv, l0_mlstm_conv, l0_mlstm_gate_b, l0_mlstm_head_norm, l0_w_out,
                           l0_ln1_g, l0_ln1_b)))
    p0.update(ffn_w_gate=l0_ffn_w_gate, ffn_w_up=l0_ffn_w_up, ffn_w_down=l0_ffn_w_down, ln2_g=l0_ln2_g, ln2_b=l0_ln2_b)
    p1 = dict(zip(_NAMES, (l1_ada_w, l1_ada_b, l1_w_in, l1_gqa_q_norm, l1_gqa_k_norm, l1_mla_q_norm, l1_mla_w_uq,
                           l1_mla_kv_norm, l1_mla_w_ukv, l1_mlstm_conv, l1_mlstm_gate_b, l1_mlstm_head_norm, l1_w_out,
                           l1_ln1_g, l1_ln1_b)))
    p1.update(router_w=l1_router_w, moe_w_gate=l1_moe_w_gate, moe_w_up=l1_moe_w_up, moe_w_down=l1_moe_w_down,
              ln2_g=l1_ln2_g, ln2_b=l1_ln2_b)
    tabs = _rope_tables(s)
    outs = []
    for bi in range(b):
        xb = x[bi]
        c_col = c[bi].reshape(d, 1)
        xb = _layer(xb, c_col, tabs, p0, moe=False)
        xb = _layer(xb, c_col, tabs, p1, moe=True)
        outs.append(xb)
    return jnp.stack(outs, axis=0)
```
